```python
import math
import jax
import jax.numpy as jnp
from jax import lax
import numpy as np

D_MODEL = 2048
BATCH = 2
SEQ = 4096
DEPTH = 4
DEC_BATCH = 8
DEC_SEQ = 8
PAST_LEN = 16384
PAGE_SIZE = 128

CONV_W = 4
RMS_EPS = 1e-6
SSD_HEAD_DIM = 64
SSD_WIDTH = D_MODEL
SSD_HEADS = SSD_WIDTH // SSD_HEAD_DIM
SSD_GROUPS = 4
SSD_HPG = SSD_HEADS // SSD_GROUPS
SSD_STATE = 128
SSD_CHUNK = 128
SSD_XBC = SSD_WIDTH + 2 * SSD_GROUPS * SSD_STATE
SB_HEAD_DIM = 128
SB_WIDTH = D_MODEL
SB_HEADS = SB_WIDTH // SB_HEAD_DIM
SB_BLOCK = 128
SB_BIAS_INIT = -5.0
LRU_WIDTH = D_MODEL
LRU_BLOCKS = 8
LRU_BLOCK_DIM = LRU_WIDTH // LRU_BLOCKS
LRU_C = 8.0
N_BRANCH = 3
D_BRANCH = D_MODEL
D_FF = -(-8 * D_MODEL // (3 * 256)) * 256
OFF_Z = 0
OFF_XBC = OFF_Z + SSD_WIDTH
OFF_DT = OFF_XBC + SSD_XBC
OFF_Q = OFF_DT + SSD_HEADS
OFF_K = OFF_Q + SB_WIDTH
OFF_V = OFF_K + SB_WIDTH
OFF_LX = OFF_V + SB_WIDTH
OFF_LG = OFF_LX + LRU_WIDTH
OFF_GATE = OFF_LG + LRU_WIDTH
N_IN = OFF_GATE + N_BRANCH * D_MODEL

kernel_name = 'hybrid_ssd_stickbreak_rglru_step'


def rms_norm(x, g, eps=RMS_EPS):
    xf = x.astype(jnp.float32)
    y = xf * lax.rsqrt(jnp.mean(jnp.square(xf), axis=-1, keepdims=True) + eps)
    return (y * g.astype(jnp.float32)).astype(x.dtype)


def causal_conv(x, buf, w, b):
    xp = jnp.concatenate([buf.astype(x.dtype), x], axis=1)
    t = x.shape[1]
    y = b + sum(w[j] * xp[:, j:j + t] for j in range(CONV_W))
    return y, xp[:, -(CONV_W - 1):]


def ssd_chunked_scan(xs, dt, la, bm, cm, h0):
    bsz, t = xs.shape[:2]
    ln = SSD_CHUNK if t % SSD_CHUNK == 0 else t
    nc = t // ln
    def chunk(a):
        return a.reshape((bsz, nc, ln) + a.shape[2:])
    xc, dtc, lac, bc, cc = chunk(xs), chunk(dt), chunk(la), chunk(bm), chunk(cm)
    acum = jnp.cumsum(lac, axis=2)
    at = jnp.transpose(acum, (0, 1, 3, 4, 2))
    tril = jnp.tril(jnp.ones((ln, ln), dtype=bool))
    seg = at[..., :, None] - at[..., None, :]
    decay = jnp.exp(jnp.where(tril, seg, -jnp.inf))
    cb = jnp.einsum('bctgn,bcsgn->bcgts', cc, bc)
    dts = jnp.transpose(dtc, (0, 1, 3, 4, 2))[..., None, :]
    scores = cb[:, :, :, None] * decay * dts
    y_diag = jnp.einsum('bcgrts,bcsgrp->bctgrp', scores, xc)
    w_state = jnp.exp(acum[:, :, -1:] - acum) * dtc
    states = jnp.einsum('bcsgn,bcsgr,bcsgrp->bcgrpn', bc, w_state, xc)
    chunk_decay = jnp.exp(acum[:, :, -1])
    def step(h, inp):
        s, d = inp
        return d[..., None, None] * h + s, h
    h_last, h_starts = lax.scan(step, h0, (jnp.moveaxis(states, 1, 0), jnp.moveaxis(chunk_decay, 1, 0)))
    h_starts = jnp.moveaxis(h_starts, 0, 1)
    y_off = jnp.einsum('bctgn,bcgrpn,bctgr->bctgrp', cc, h_starts, jnp.exp(acum))
    y = (y_diag + y_off).reshape(bsz, t, SSD_GROUPS, SSD_HPG, SSD_HEAD_DIM)
    return y, h_last


def ssd_mixer(z, xbc_raw, dt_raw, conv_buf, h0, conv_w, conv_b, dt_bias, a_log, d_skip, norm_g):
    bsz, t = z.shape[:2]
    f32 = jnp.float32
    xbc, new_buf = causal_conv(xbc_raw, conv_buf, conv_w, conv_b)
    xbc = jax.nn.silu(xbc.astype(f32))
    gn = SSD_GROUPS * SSD_STATE
    xs = xbc[..., :SSD_WIDTH].reshape(bsz, t, SSD_GROUPS, SSD_HPG, SSD_HEAD_DIM)
    bm = xbc[..., SSD_WIDTH:SSD_WIDTH + gn].reshape(bsz, t, SSD_GROUPS, SSD_STATE)
    cm = xbc[..., SSD_WIDTH + gn:].reshape(bsz, t, SSD_GROUPS, SSD_STATE)
    dt = jax.nn.softplus(dt_raw.astype(f32) + dt_bias.astype(f32)).reshape(bsz, t, SSD_GROUPS, SSD_HPG)
    a = -jnp.exp(a_log.astype(f32)).reshape(SSD_GROUPS, SSD_HPG)
    h0g = h0.astype(f32).reshape(bsz, SSD_GROUPS, SSD_HPG, SSD_HEAD_DIM, SSD_STATE)
    y, h_last = ssd_chunked_scan(xs, dt, dt * a, bm, cm, h0g)
    y = y + d_skip.astype(f32).reshape(SSD_GROUPS, SSD_HPG)[:, :, None] * xs
    y = y.reshape(bsz, t, SSD_WIDTH) * jax.nn.silu(z.astype(f32))
    y = rms_norm(y, norm_g)
    return y.astype(z.dtype), h_last.reshape(bsz, SSD_HEADS, SSD_HEAD_DIM, SSD_STATE), new_buf


def stick_breaking_block(q, k, v, bias):
    tq, s = q.shape[1], k.shape[1]
    z = jnp.einsum('bqhd,bshd->bhqs', q, k, preferred_element_type=jnp.float32) * (SB_HEAD_DIM ** -0.5)
    z = z + bias.astype(jnp.float32)[None, :, None, None]
    q_pos = s - tq + jnp.arange(tq)
    mask = jnp.arange(s)[None, :] < q_pos[:, None]
    log_1m = jnp.where(mask, jax.nn.log_sigmoid(-z), 0.0)
    suffix = lax.cumsum(log_1m, axis=3, reverse=True) - log_1m
    w = jnp.where(mask, jnp.exp(jax.nn.log_sigmoid(z) + suffix), 0.0)
    return jnp.einsum('bhqs,bshd->bqhd', w.astype(v.dtype), v)


def stick_breaking_sweep(q, k, v, bias):
    tq, s = q.shape[1], k.shape[1]
    outs = []
    for qs in range(0, tq, SB_BLOCK):
        qe = min(qs + SB_BLOCK, tq)
        ke = s - tq + qe
        outs.append(stick_breaking_block(q[:, qs:qe], k[:, :ke], v[:, :ke], bias))
    return jnp.concatenate(outs, axis=1)


def _linear_combine(l, r):
    a1, b1 = l
    a2, b2 = r
    return a1 * a2, a2 * b1 + b2


def rg_lru_mixer(x_raw, gate_raw, conv_buf, h0, conv_w, conv_b, w_a, b_a, w_x, b_x, lam):
    bsz, t = x_raw.shape[:2]
    f32 = jnp.float32
    xc, new_buf = causal_conv(x_raw, conv_buf, conv_w, conv_b)
    xf = xc.astype(f32)
    xb = xf.reshape(bsz, t, LRU_BLOCKS, LRU_BLOCK_DIM)
    r = jax.nn.sigmoid(jnp.einsum('btki,kij->btkj', xb, w_a.astype(f32)).reshape(bsz, t, LRU_WIDTH) + b_a)
    i = jax.nn.sigmoid(jnp.einsum('btki,kij->btkj', xb, w_x.astype(f32)).reshape(bsz, t, LRU_WIDTH) + b_x)
    log_a = -LRU_C * r * jax.nn.softplus(-lam.astype(f32))
    a = jnp.exp(log_a)
    b = jnp.sqrt(-jnp.expm1(2.0 * log_a)) * (i * xf)
    a_cum, b_cum = lax.associative_scan(_linear_combine, (a, b), axis=1)
    h = a_cum * h0.astype(f32)[:, None, :] + b_cum
    y = jax.nn.gelu(gate_raw.astype(f32)) * h
    return y.astype(x_raw.dtype), h[:, -1], new_buf


def hybrid_layer(x, c, past_k, past_v, ssd_h0, ssd_buf, lru_h0, lru_buf, p):
    bsz, t, _ = x.shape
    mod = jax.nn.silu(c) @ p['w_ada'] + p['b_ada']
    sh1, sc1, g1, sh2, sc2, g2 = jnp.split(mod[:, None, :], 6, axis=-1)
    u = rms_norm(x, p['g_pre_mix']) * (1.0 + sc1) + sh1
    proj = u @ p['w_in']
    y_ssd, ssd_h, ssd_buf_new = ssd_mixer(proj[..., OFF_Z:OFF_XBC], proj[..., OFF_XBC:OFF_DT], proj[..., OFF_DT:OFF_Q], ssd_buf, ssd_h0, p['ssd_conv_w'], p['ssd_conv_b'], p['ssd_dt_bias'], p['ssd_a_log'], p['ssd_d'], p['ssd_norm_g'])
    q = proj[..., OFF_Q:OFF_K].reshape(bsz, t, SB_HEADS, SB_HEAD_DIM)
    k_new = proj[..., OFF_K:OFF_V].reshape(bsz, t, SB_HEADS, SB_HEAD_DIM)
    v_new = proj[..., OFF_V:OFF_LX].reshape(bsz, t, SB_HEADS, SB_HEAD_DIM)
    if past_k is None:
        k_all, v_all = k_new, v_new
    else:
        k_all = jnp.concatenate([past_k.astype(k_new.dtype), k_new], axis=1)
        v_all = jnp.concatenate([past_v.astype(v_new.dtype), v_new], axis=1)
    y_sb = stick_breaking_sweep(q, k_all, v_all, p['sb_bias']).reshape(bsz, t, SB_WIDTH)
    y_lru, lru_h, lru_buf_new = rg_lru_mixer(proj[..., OFF_LX:OFF_LG], proj[..., OFF_LG:OFF_GATE], lru_buf, lru_h0, p['lru_conv_w'], p['lru_conv_b'], p['lru_w_a'], p['lru_b_a'], p['lru_w_x'], p['lru_b_x'], p['lru_lambda'])
    gates = jax.nn.sigmoid(proj[..., OFF_GATE:].astype(jnp.float32)).reshape(bsz, t, N_BRANCH, D_MODEL)
    branches = jnp.stack([y_ssd, y_sb.astype(y_ssd.dtype), y_lru.astype(y_ssd.dtype)], axis=2)
    merged = jnp.sum(gates.astype(x.dtype) * jnp.einsum('btnd,ndm->btnm', branches, p['w_branch']), axis=2)
    mix = merged @ p['w_out']
    x = x + g1 * rms_norm(mix, p['g_post_mix'])
    u2 = rms_norm(x, p['g_pre_ffn']) * (1.0 + sc2) + sh2
    gate_up = u2 @ p['w_up']
    hidden = jax.nn.silu(gate_up[..., :D_FF]) * gate_up[..., D_FF:]
    x = x + g2 * rms_norm(hidden @ p['w_down'], p['g_post_ffn'])
    return x, k_new, v_new, ssd_h, ssd_buf_new, lru_h, lru_buf_new


def setup_inputs(seed: int = 0) -> dict:
    key = jax.random.key(seed)
    keys = iter(jax.random.split(key, 48))
    f32 = jnp.float32
    def normal(shape, scale):
        return jax.random.normal(next(keys), shape, f32) * scale
    def uniform(shape, lo, hi):
        return jax.random.uniform(next(keys), shape, f32, minval=lo, maxval=hi)
    n_pages = PAST_LEN // PAGE_SIZE
    n_used = DEC_BATCH * n_pages
    n_pool = -(-n_used * 5 // 4)
    page_table = jax.random.permutation(next(keys), n_pool)[:n_used].reshape(DEC_BATCH, n_pages).astype(jnp.int32)
    dt0 = jnp.exp(uniform((DEPTH, SSD_HEADS), math.log(1e-3), math.log(1e-1)))
    a_init = uniform((DEPTH, LRU_WIDTH), 0.9, 0.999)
    return {
        'x_prompt': normal((BATCH, SEQ, D_MODEL), 1.0),
        'x_sample': normal((DEC_BATCH, DEC_SEQ, D_MODEL), 1.0),
        'c_prompt': normal((BATCH, D_MODEL), 1.0),
        'c_sample': normal((DEC_BATCH, D_MODEL), 1.0),
        'cache_k': normal((DEPTH, n_pool, PAGE_SIZE, SB_HEADS, SB_HEAD_DIM), 1.0),
        'cache_v': normal((DEPTH, n_pool, PAGE_SIZE, SB_HEADS, SB_HEAD_DIM), 1.0),
        'page_table': page_table,
        'state_ssd': normal((DEPTH, DEC_BATCH, SSD_HEADS, SSD_HEAD_DIM, SSD_STATE), 0.5),
        'state_ssd_conv': normal((DEPTH, DEC_BATCH, CONV_W - 1, SSD_XBC), 1.0),
        'state_lru': normal((DEPTH, DEC_BATCH, LRU_WIDTH), 1.0),
        'state_lru_conv': normal((DEPTH, DEC_BATCH, CONV_W - 1, LRU_WIDTH), 1.0),
        'w_ada': normal((DEPTH, D_MODEL, 6 * D_MODEL), D_MODEL ** -0.5),
        'b_ada': normal((DEPTH, 6 * D_MODEL), 0.01),
        'g_pre_mix': 1.0 + normal((DEPTH, D_MODEL), 0.02),
        'g_post_mix': 1.0 + normal((DEPTH, D_MODEL), 0.02),
        'g_pre_ffn': 1.0 + normal((DEPTH, D_MODEL), 0.02),
        'g_post_ffn': 1.0 + normal((DEPTH, D_MODEL), 0.02),
        'w_in': normal((DEPTH, D_MODEL, N_IN), D_MODEL ** -0.5),
        'ssd_conv_w': normal((DEPTH, CONV_W, SSD_XBC), CONV_W ** -0.5),
        'ssd_conv_b': normal((DEPTH, SSD_XBC), 0.01),
        'ssd_dt_bias': dt0 + jnp.log(-jnp.expm1(-dt0)),
        'ssd_a_log': jnp.log(uniform((DEPTH, SSD_HEADS), 1.0, 16.0)),
        'ssd_d': 1.0 + normal((DEPTH, SSD_HEADS), 0.1),
        'ssd_norm_g': 1.0 + normal((DEPTH, SSD_WIDTH), 0.02),
        'sb_bias': SB_BIAS_INIT + normal((DEPTH, SB_HEADS), 0.1),
        'lru_conv_w': normal((DEPTH, CONV_W, LRU_WIDTH), CONV_W ** -0.5),
        'lru_conv_b': normal((DEPTH, LRU_WIDTH), 0.01),
        'lru_w_a': normal((DEPTH, LRU_BLOCKS, LRU_BLOCK_DIM, LRU_BLOCK_DIM), LRU_BLOCK_DIM ** -0.5),
        'lru_b_a': normal((DEPTH, LRU_WIDTH), 0.01),
        'lru_w_x': normal((DEPTH, LRU_BLOCKS, LRU_BLOCK_DIM, LRU_BLOCK_DIM), LRU_BLOCK_DIM ** -0.5),
        'lru_b_x': normal((DEPTH, LRU_WIDTH), 0.01),
        'lru_lambda': jnp.log(a_init) - jnp.log1p(-a_init),
        'w_branch': normal((DEPTH, N_BRANCH, D_BRANCH, D_MODEL), D_BRANCH ** -0.5),
        'w_out': normal((DEPTH, D_MODEL, D_MODEL), D_MODEL ** -0.5),
        'w_up': normal((DEPTH, D_MODEL, 2 * D_FF), D_MODEL ** -0.5),
        'w_down': normal((DEPTH, D_FF, D_MODEL), D_FF ** -0.5),
    }


def reference(x_prompt, x_sample, c_prompt, c_sample, cache_k, cache_v, page_table, state_ssd, state_ssd_conv, state_lru, state_lru_conv, w_ada, b_ada, g_pre_mix, g_post_mix, g_pre_ffn, g_post_ffn, w_in, ssd_conv_w, ssd_conv_b, ssd_dt_bias, ssd_a_log, ssd_d, ssd_norm_g, sb_bias, lru_conv_w, lru_conv_b, lru_w_a, lru_b_a, lru_w_x, lru_b_x, lru_lambda, w_branch, w_out, w_up, w_down):
    n_seq, n_pages = page_table.shape
    past_len = n_pages * cache_k.shape[2]
    bp = x_prompt.shape[0]
    dtp = x_prompt.dtype
    xp, xs = x_prompt, x_sample
    kp, vp, ksm, vsm = [], [], [], []
    hp, hsm, cvp, cvs = [], [], [], []
    lhp, lhs, lcp, lcs = [], [], [], []
    for l in range(DEPTH):
        p = {'w_ada': w_ada[l], 'b_ada': b_ada[l], 'g_pre_mix': g_pre_mix[l], 'g_post_mix': g_post_mix[l],
             'g_pre_ffn': g_pre_ffn[l], 'g_post_ffn': g_post_ffn[l], 'w_in': w_in[l],
             'ssd_conv_w': ssd_conv_w[l], 'ssd_conv_b': ssd_conv_b[l], 'ssd_dt_bias': ssd_dt_bias[l],
             'ssd_a_log': ssd_a_log[l], 'ssd_d': ssd_d[l], 'ssd_norm_g': ssd_norm_g[l], 'sb_bias': sb_bias[l],
             'lru_conv_w': lru_conv_w[l], 'lru_conv_b': lru_conv_b[l], 'lru_w_a': lru_w_a[l], 'lru_b_a': lru_b_a[l],
             'lru_w_x': lru_w_x[l], 'lru_b_x': lru_b_x[l], 'lru_lambda': lru_lambda[l],
             'w_branch': w_branch[l], 'w_out': w_out[l], 'w_up': w_up[l], 'w_down': w_down[l]}
        xp, k_p, v_p, h_p, c_p, lh_p, lc_p = hybrid_layer(
            xp, c_prompt, None, None,
            jnp.zeros((bp, SSD_HEADS, SSD_HEAD_DIM, SSD_STATE), dtp), jnp.zeros((bp, CONV_W - 1, SSD_XBC), dtp),
            jnp.zeros((bp, LRU_WIDTH), dtp), jnp.zeros((bp, CONV_W - 1, LRU_WIDTH), dtp), p)
        past_k = cache_k[l][page_table].reshape(n_seq, past_len, SB_HEADS, SB_HEAD_DIM)
        past_v = cache_v[l][page_table].reshape(n_seq, past_len, SB_HEADS, SB_HEAD_DIM)
        xs, k_s, v_s, h_s, c_s, lh_s, lc_s = hybrid_layer(
            xs, c_sample, past_k, past_v, state_ssd[l], state_ssd_conv[l], state_lru[l], state_lru_conv[l], p)
        kp.append(k_p); vp.append(v_p); ksm.append(k_s); vsm.append(v_s)
        hp.append(h_p); hsm.append(h_s); cvp.append(c_p); cvs.append(c_s)
        lhp.append(lh_p); lhs.append(lh_s); lcp.append(lc_p); lcs.append(lc_s)
    return (xp, xs, jnp.stack(kp), jnp.stack(vp), jnp.stack(ksm), jnp.stack(vsm), jnp.stack(hp), jnp.stack(hsm), jnp.stack(cvp), jnp.stack(cvs), jnp.stack(lhp), jnp.stack(lhs), jnp.stack(lcp), jnp.stack(lcs))
```

```python
import functools

import jax
import jax.numpy as jnp
from jax import lax
from jax.experimental import pallas as pl
from jax.experimental.pallas import tpu as pltpu

F32 = jnp.float32
BF16 = jnp.bfloat16

D_MODEL = 2048
DEPTH = 4
CONV_W = 4
RMS_EPS = 1e-6
SSD_HEAD_DIM = 64
SSD_HEADS = 32
SSD_GROUPS = 4
SSD_STATE = 128
SSD_CHUNK = 128
SSD_XBC = D_MODEL + 2 * SSD_GROUPS * SSD_STATE
SSD_PAIRS = SSD_HEADS // 2
SB_HEADS = 16
SB_HEAD_DIM = 128
SB_BLOCK = 128
LRU_BLOCKS = 8
LRU_BLOCK_DIM = 256
LRU_C = 8.0
N_BRANCH = 3
D_FF = 5632
LANES = 128

COL_Z = 0
COL_Q = COL_Z + D_MODEL
COL_XBC = COL_Q + D_MODEL
COL_LX = COL_XBC + SSD_XBC
COL_LG = COL_LX + D_MODEL
COL_GATE = COL_LG + D_MODEL
N_MAIN = COL_GATE + N_BRANCH * D_MODEL
COL_K = N_MAIN
COL_V = COL_K + D_MODEL
COL_DT = COL_V + D_MODEL
N_IN_PAD = COL_DT + LANES

VMEM_LIMIT = 48 * 1024 * 1024


def _cparams(sem):
    return pltpu.CompilerParams(dimension_semantics=sem, vmem_limit_bytes=VMEM_LIMIT)


def _dot(a, b):
    return jnp.dot(a, b, preferred_element_type=F32)


def _dot_nt(a, b):
    return lax.dot_general(a, b, (((1,), (1,)), ((), ())), preferred_element_type=F32)


def _dot_tn(a, b):
    return lax.dot_general(a, b, (((0,), (0,)), ((), ())), preferred_element_type=F32)


def _sigmoid(x):
    return 1.0 / (1.0 + jnp.exp(-x))


def _silu(x):
    return x * _sigmoid(x)


def _softplus(x):
    return jnp.maximum(x, 0.0) + jnp.log1p(jnp.exp(-jnp.abs(x)))


def _gelu_tanh(x):
    return 0.5 * x * (1.0 + jnp.tanh(0.7978845608028654 * (x + 0.044715 * (x * x * x))))


def _split_dot(x, w, terms):
    acc = None
    rem = x
    for t in range(terms):
        piece = rem.astype(BF16)
        part = _dot(piece, w)
        acc = part if acc is None else acc + part
        if t + 1 < terms:
            rem = rem - piece.astype(F32)
    return acc


def _split_dot_left(w, x, terms):
    acc = None
    rem = x
    for t in range(terms):
        piece = rem.astype(BF16)
        part = _dot(w, piece)
        acc = part if acc is None else acc + part
        if t + 1 < terms:
            rem = rem - piece.astype(F32)
    return acc


def _mm_kernel(*refs, pre_silu, has_bias):
    if has_bias:
        a_ref, w_ref, b_ref, o_ref = refs
    else:
        a_ref, w_ref, o_ref = refs
    a = a_ref[...]
    if pre_silu:
        a = _silu(a.astype(F32))
    acc = _dot(a.astype(BF16), w_ref[...])
    if has_bias:
        acc = acc + b_ref[...]
    o_ref[...] = acc.astype(o_ref.dtype)


def _mm(a, w, col0, ncols, tn, tm, bias=None, pre_silu=False, out_dtype=F32):
    m, k = a.shape
    assert col0 % tn == 0 and ncols % tn == 0 and m % tm == 0
    cb = col0 // tn
    in_specs = [pl.BlockSpec((tm, k), lambda j, i: (i, 0)),
                pl.BlockSpec((k, tn), lambda j, i: (0, j + cb))]
    args = [a, w]
    if bias is not None:
        in_specs.append(pl.BlockSpec((1, tn), lambda j, i: (0, j + cb)))
        args.append(bias)
    return pl.pallas_call(
        functools.partial(_mm_kernel, pre_silu=pre_silu, has_bias=bias is not None),
        grid=(ncols // tn, m // tm),
        in_specs=in_specs,
        out_specs=pl.BlockSpec((tm, tn), lambda j, i: (i, j)),
        out_shape=jax.ShapeDtypeStruct((m, ncols), out_dtype),
        compiler_params=_cparams(("parallel", "parallel")),
        name="mm",
    )(*args)


def _mod_spec(mod, bpb):
    _, r, d = mod.shape
    return pl.BlockSpec((1, r, d), lambda i, *_: (i // bpb, 0, 0))


def _norm_mod_kernel(x_ref, g_ref, sc_ref, sh_ref, o_ref):
    x = x_ref[...]
    ms = jnp.mean(x * x, axis=-1, keepdims=True)
    y = x * lax.rsqrt(ms + RMS_EPS) * g_ref[...]
    o_ref[...] = (y * (1.0 + sc_ref[0]) + sh_ref[0]).astype(o_ref.dtype)


def _norm_mod(x, g, scale, shift, tm, bpb):
    m, d = x.shape
    return pl.pallas_call(
        _norm_mod_kernel,
        grid=(m // tm,),
        in_specs=[pl.BlockSpec((tm, d), lambda i: (i, 0)),
                  pl.BlockSpec((1, d), lambda i: (0, 0)),
                  _mod_spec(scale, bpb), _mod_spec(shift, bpb)],
        out_specs=pl.BlockSpec((tm, d), lambda i: (i, 0)),
        out_shape=jax.ShapeDtypeStruct((m, d), BF16),
        compiler_params=_cparams(("parallel",)),
        name="norm_mod",
    )(x, g, scale, shift)


def _mm_norm_res_kernel(a_ref, w_ref, x_ref, g_ref, gm_ref, o_ref, acc_ref, *, nk):
    k = pl.program_id(1)

    @pl.when(k == 0)
    def _():
        acc_ref[...] = jnp.zeros_like(acc_ref)

    acc_ref[...] += _dot(a_ref[...], w_ref[...])

    @pl.when(k == nk - 1)
    def _():
        mix = acc_ref[...]
        ms = jnp.mean(mix * mix, axis=-1, keepdims=True)
        o_ref[...] = x_ref[...] + gm_ref[0] * (mix * lax.rsqrt(ms + RMS_EPS) * g_ref[...])


def _mm_norm_res(a, w, x, g, gmod, tm, tk, bpb):
    m, kdim = a.shape
    d = w.shape[1]
    nk = kdim // tk
    return pl.pallas_call(
        functools.partial(_mm_norm_res_kernel, nk=nk),
        grid=(m // tm, nk),
        in_specs=[pl.BlockSpec((tm, tk), lambda i, k: (i, k)),
                  pl.BlockSpec((tk, d), lambda i, k: (k, 0)),
                  pl.BlockSpec((tm, d), lambda i, k: (i, 0)),
                  pl.BlockSpec((1, d), lambda i, k: (0, 0)),
                  _mod_spec(gmod, bpb)],
        out_specs=pl.BlockSpec((tm, d), lambda i, k: (i, 0)),
        out_shape=jax.ShapeDtypeStruct((m, d), F32),
        scratch_shapes=[pltpu.VMEM((tm, d), F32)],
        compiler_params=_cparams(("parallel", "arbitrary")),
        name="mm_norm_res",
    )(a, w, x, g, gmod)


def _mm_glu_kernel(a_ref, wg_ref, wu_ref, o_ref):
    a = a_ref[...]
    gate = _dot(a, wg_ref[...])
    up = _dot(a, wu_ref[...])
    o_ref[...] = (_silu(gate) * up).astype(o_ref.dtype)


def _mm_glu(a, w_up, tm, tn):
    m, k = a.shape
    nb = D_FF // tn
    return pl.pallas_call(
        _mm_glu_kernel,
        grid=(nb, m // tm),
        in_specs=[pl.BlockSpec((tm, k), lambda j, i: (i, 0)),
                  pl.BlockSpec((k, tn), lambda j, i: (0, j)),
                  pl.BlockSpec((k, tn), lambda j, i: (0, j + nb))],
        out_specs=pl.BlockSpec((tm, tn), lambda j, i: (i, j)),
        out_shape=jax.ShapeDtypeStruct((m, D_FF), BF16),
        compiler_params=_cparams(("parallel", "parallel")),
        name="mm_glu",
    )(a, w_up, w_up)


def _merge_kernel(y0_ref, y1_ref, y2_ref, w_ref, g0_ref, g1_ref, g2_ref, o_ref):
    acc = _sigmoid(g0_ref[...]) * _dot(y0_ref[...].astype(BF16), w_ref[0])
    acc = acc + _sigmoid(g1_ref[...]) * _dot(y1_ref[...].astype(BF16), w_ref[1])
    acc = acc + _sigmoid(g2_ref[...]) * _dot(y2_ref[...].astype(BF16), w_ref[2])
    o_ref[...] = acc.astype(o_ref.dtype)


def _merge(y_ssd, y_sb, y_lru, proj, w_branch, tm, tn):
    m, d = y_ssd.shape
    gb = COL_GATE // tn
    nb = d // tn
    y_spec = pl.BlockSpec((tm, d), lambda i, j: (i, 0))
    return pl.pallas_call(
        _merge_kernel,
        grid=(m // tm, nb),
        in_specs=[y_spec, y_spec, y_spec,
                  pl.BlockSpec((N_BRANCH, d, tn), lambda i, j: (0, 0, j)),
                  pl.BlockSpec((tm, tn), lambda i, j: (i, gb + j)),
                  pl.BlockSpec((tm, tn), lambda i, j: (i, gb + nb + j)),
                  pl.BlockSpec((tm, tn), lambda i, j: (i, gb + 2 * nb + j))],
        out_specs=pl.BlockSpec((tm, tn), lambda i, j: (i, j)),
        out_shape=jax.ShapeDtypeStruct((m, d), BF16),
        compiler_params=_cparams(("parallel", "parallel")),
        name="merge",
    )(y_ssd, y_sb, y_lru, w_branch, proj, proj, proj)


def _conv_from_scratch(xp_ref, s, raw, w, b, rows):
    xp_ref[s, 8:8 + rows, :] = raw
    y = b + w[3:4] * raw
    for j in range(CONV_W - 1):
        y = y + w[j:j + 1] * xp_ref[s, 5 + j:5 + j + rows, :]
    if rows >= 8:
        xp_ref[s, 0:8, :] = raw[rows - 8:rows]
    return y


def _ssd_kernel(*refs, rows, has_init):
    if has_init:
        (dtb_ref, alog_ref, dsk_ref, x_ref, bm_ref, cm_ref, dt_ref, cwx_ref, cwb_ref, cwc_ref, cbx_ref, cbb_ref, cbc_ref,
         bufx_ref, bufb_ref, bufc_ref, h0_ref, y_ref, hout_ref, h_ref, xp_ref) = refs
    else:
        (dtb_ref, alog_ref, dsk_ref, x_ref, bm_ref, cm_ref, dt_ref, cwx_ref, cwb_ref, cwc_ref, cbx_ref, cbb_ref, cbc_ref,
         y_ref, hout_ref, h_ref, xp_ref) = refs
    p = pl.program_id(1)
    r = pl.program_id(2)
    nr = pl.num_programs(2)
    ln = SSD_CHUNK

    @pl.when(r == 0)
    def _():
        if has_init:
            h_ref[...] = h0_ref[0, 0]
            xp_ref[0, 5:8, :] = bufx_ref[0]
            xp_ref[1, 5:8, :] = bufb_ref[0]
            xp_ref[2, 5:8, :] = bufc_ref[0]
        else:
            h_ref[...] = jnp.zeros_like(h_ref)
            xp_ref[:, 0:8, :] = jnp.zeros((3, 8, LANES), F32)

    xs_all = _silu(_conv_from_scratch(xp_ref, 0, x_ref[...], cwx_ref[...], cbx_ref[...], rows))
    bm_all = _silu(_conv_from_scratch(xp_ref, 1, bm_ref[...], cwb_ref[...], cbb_ref[...], rows))
    cm_all = _silu(_conv_from_scratch(xp_ref, 2, cm_ref[...], cwc_ref[...], cbc_ref[...], rows))
    dt_all = dt_ref[...]

    lane = lax.broadcasted_iota(jnp.int32, (ln, LANES), 1)
    row = lax.broadcasted_iota(jnp.int32, (ln, LANES), 0)
    first_half = lane < SSD_HEAD_DIM
    tril = row >= lane
    eye = row == lane
    tri_bf = jnp.where(tril, 1.0, 0.0).astype(BF16)

    h0i = 2 * p
    h1i = 2 * p + 1
    a0 = -jnp.exp(jnp.zeros((1, LANES), F32) + alog_ref[h0i])
    a1 = -jnp.exp(jnp.zeros((1, LANES), F32) + alog_ref[h1i])
    dskip = jnp.where(first_half[0:1], dsk_ref[h0i], dsk_ref[h1i])

    n_chunks = max(rows // ln, 1)
    for c in range(n_chunks):
        if rows >= ln:
            xs = xs_all[c * ln:(c + 1) * ln]
            bm = bm_all[c * ln:(c + 1) * ln]
            cm = cm_all[c * ln:(c + 1) * ln]
            dtr = dt_all[c * ln:(c + 1) * ln]
            valid = None
        else:
            pad = jnp.zeros((ln - rows, LANES), F32)
            xs = jnp.concatenate([xs_all, pad], axis=0)
            bm = jnp.concatenate([bm_all, pad], axis=0)
            cm = jnp.concatenate([cm_all, pad], axis=0)
            dtr = jnp.concatenate([dt_all, pad], axis=0)
            valid = row < rows

        dtr0 = jnp.sum(jnp.where(lane == h0i, dtr, 0.0), axis=1, keepdims=True)
        dtr1 = jnp.sum(jnp.where(lane == h1i, dtr, 0.0), axis=1, keepdims=True)
        dt0 = _softplus(jnp.broadcast_to(dtr0, (ln, LANES)) + dtb_ref[h0i])
        dt1 = _softplus(jnp.broadcast_to(dtr1, (ln, LANES)) + dtb_ref[h1i])
        if valid is not None:
            dt0 = jnp.where(valid, dt0, 0.0)
            dt1 = jnp.where(valid, dt1, 0.0)
        la = jnp.concatenate([dt0 * a0, dt1 * a1], axis=1)
        acum = _split_dot_left(tri_bf, la, 3)
        ac0 = acum[:, :LANES]
        ac1 = acum[:, LANES:]
        acp = jnp.where(first_half, ac0, ac1)
        dtp = jnp.where(first_half, dt0, dt1)
        last0 = ac0[ln - 1:ln]
        last1 = ac1[ln - 1:ln]
        lastp = acp[ln - 1:ln]

        cb = _dot_nt(cm.astype(BF16), bm.astype(BF16))
        xs_bf = xs.astype(BF16)

        def scores(ac, dt):
            arow = jnp.sum(jnp.where(eye, ac, 0.0), axis=0, keepdims=True)
            dtrow = jnp.sum(jnp.where(eye, dt, 0.0), axis=0, keepdims=True)
            decay = jnp.exp(jnp.where(tril, ac - arow, -jnp.inf))
            return (cb * decay * dtrow).astype(BF16)

        y_diag = jnp.where(first_half, _dot(scores(ac0, dt0), xs_bf), _dot(scores(ac1, dt1), xs_bf))
        h_prev = h_ref[...]
        y_off = jnp.exp(acp) * _dot_nt(cm.astype(BF16), h_prev.astype(BF16))
        y = y_diag + y_off + dskip * xs
        y_ref[c * ln:c * ln + min(rows, ln), :] = y[:min(rows, ln)]

        w_state = jnp.exp(lastp - acp) * dtp
        states = _dot_tn((xs * w_state).astype(BF16), bm.astype(BF16))
        cdec = jnp.where(row < SSD_HEAD_DIM, jnp.broadcast_to(last0, (ln, LANES)), jnp.broadcast_to(last1, (ln, LANES)))
        h_ref[...] = jnp.exp(cdec) * h_prev + states

    @pl.when(r == nr - 1)
    def _():
        hout_ref[0, 0] = h_ref[...]


def _ssd_scan(proj, dt_raw, bsz, t, rows, conv_w, conv_b, dt_bias, a_log, d_skip, conv_buf=None, h0=None):
    m = bsz * t
    nrb = t // rows
    has_init = h0 is not None
    cx = COL_XBC // LANES
    cbm = cx + D_MODEL // LANES
    ccm = cbm + SSD_GROUPS
    hpg = SSD_PAIRS // SSD_GROUPS

    def rowmap(off_fn):
        return lambda b, p, r, *_: (b * nrb + r, off_fn(p))

    def wmap(off_fn):
        return lambda b, p, r, *_: (0, off_fn(p))

    def bufmap(off_fn):
        return lambda b, p, r, *_: (b, 0, off_fn(p))

    fx = lambda p: p
    fb = lambda p: D_MODEL // LANES + p // hpg
    fc = lambda p: D_MODEL // LANES + SSD_GROUPS + p // hpg
    in_specs = [
        pl.BlockSpec((rows, LANES), rowmap(lambda p: cx + p)),
        pl.BlockSpec((rows, LANES), rowmap(lambda p: cbm + p // hpg)),
        pl.BlockSpec((rows, LANES), rowmap(lambda p: ccm + p // hpg)),
        pl.BlockSpec((rows, LANES), rowmap(lambda p: 0)),
        pl.BlockSpec((CONV_W, LANES), wmap(fx)), pl.BlockSpec((CONV_W, LANES), wmap(fb)), pl.BlockSpec((CONV_W, LANES), wmap(fc)),
        pl.BlockSpec((1, LANES), wmap(fx)), pl.BlockSpec((1, LANES), wmap(fb)), pl.BlockSpec((1, LANES), wmap(fc)),
    ]
    args = [proj, proj, proj, dt_raw, conv_w, conv_w, conv_w, conv_b, conv_b, conv_b]
    if has_init:
        in_specs += [pl.BlockSpec((1, CONV_W - 1, LANES), bufmap(fx)),
                     pl.BlockSpec((1, CONV_W - 1, LANES), bufmap(fb)),
                     pl.BlockSpec((1, CONV_W - 1, LANES), bufmap(fc)),
                     pl.BlockSpec((1, 1, LANES, SSD_STATE), lambda b, p, r, *_: (b, p, 0, 0))]
        args += [conv_buf, conv_buf, conv_buf, h0]
    grid_spec = pltpu.PrefetchScalarGridSpec(
        num_scalar_prefetch=3,
        grid=(bsz, SSD_PAIRS, nrb),
        in_specs=in_specs,
        out_specs=[pl.BlockSpec((rows, LANES), rowmap(lambda p: p)),
                   pl.BlockSpec((1, 1, LANES, SSD_STATE), lambda b, p, r, *_: (b, p, 0, 0))],
        scratch_shapes=[pltpu.VMEM((LANES, SSD_STATE), F32), pltpu.VMEM((3, rows + 8, LANES), F32)],
    )
    return pl.pallas_call(
        functools.partial(_ssd_kernel, rows=rows, has_init=has_init),
        grid_spec=grid_spec,
        out_shape=[jax.ShapeDtypeStruct((m, D_MODEL), F32),
                   jax.ShapeDtypeStruct((bsz, SSD_PAIRS, LANES, SSD_STATE), F32)],
        compiler_params=_cparams(("parallel", "parallel", "arbitrary")),
        name="ssd_scan",
    )(dt_bias, a_log, d_skip, *args)


def _ssd_gate_kernel(y_ref, z_ref, g_ref, o_ref):
    y = y_ref[...] * _silu(z_ref[...])
    ms = jnp.mean(y * y, axis=-1, keepdims=True)
    o_ref[...] = (y * lax.rsqrt(ms + RMS_EPS) * g_ref[...]).astype(o_ref.dtype)


def _ssd_gate(y, proj, norm_g, tm):
    m, d = y.shape
    return pl.pallas_call(
        _ssd_gate_kernel,
        grid=(m // tm,),
        in_specs=[pl.BlockSpec((tm, d), lambda i: (i, 0)),
                  pl.BlockSpec((tm, d), lambda i: (i, COL_Z // D_MODEL)),
                  pl.BlockSpec((1, d), lambda i: (0, 0))],
        out_specs=pl.BlockSpec((tm, d), lambda i: (i, 0)),
        out_shape=jax.ShapeDtypeStruct((m, d), BF16),
        compiler_params=_cparams(("parallel",)),
        name="ssd_gate",
    )(y, proj, norm_g)


def _lru_kernel(*refs, rows, has_init):
    if has_init:
        (x_ref, g_ref, cw_ref, cb_ref, wa_ref, wx_ref, ba_ref, bx_ref, lam_ref, buf_ref, h0_ref,
         y_ref, hout_ref, h_ref, xp_ref) = refs
    else:
        (x_ref, g_ref, cw_ref, cb_ref, wa_ref, wx_ref, ba_ref, bx_ref, lam_ref,
         y_ref, hout_ref, h_ref, xp_ref) = refs
    r = pl.program_id(2)
    nr = pl.num_programs(2)
    w = LRU_BLOCK_DIM

    @pl.when(r == 0)
    def _():
        if has_init:
            h_ref[...] = h0_ref[0]
            xp_ref[0, 5:8, :] = buf_ref[0]
        else:
            h_ref[...] = jnp.zeros_like(h_ref)
            xp_ref[0, 0:8, :] = jnp.zeros((8, w), F32)

    xc = _conv_from_scratch(xp_ref, 0, x_ref[...], cw_ref[...], cb_ref[...], rows)
    xb = xc.astype(BF16)
    rg = _sigmoid(_dot(xb, wa_ref[0]) + ba_ref[...])
    ig = _sigmoid(_dot(xb, wx_ref[0]) + bx_ref[...])
    log_a = -LRU_C * rg * _softplus(-lam_ref[...])
    a = jnp.exp(log_a)
    b = jnp.sqrt(-jnp.tanh(log_a) * (a * a + 1.0)) * (ig * xc)

    row = lax.broadcasted_iota(jnp.int32, (rows, w), 0)
    d = 1
    while d < rows:
        keep = row >= d
        a_sh = jnp.where(keep, pltpu.roll(a, d, 0), 1.0)
        b_sh = jnp.where(keep, pltpu.roll(b, d, 0), 0.0)
        b = a * b_sh + b
        a = a * a_sh
        d *= 2
    h = a * h_ref[...] + b
    h_ref[...] = h[rows - 1:rows]
    y_ref[...] = (_gelu_tanh(g_ref[...]) * h).astype(y_ref.dtype)

    @pl.when(r == nr - 1)
    def _():
        hout_ref[0] = h[rows - 1:rows]


def _lru(proj, bsz, t, rows, conv_w, conv_b, w_a, w_x, b_a, b_x, lam, conv_buf=None, h0=None):
    m = bsz * t
    nrb = t // rows
    w = LRU_BLOCK_DIM
    has_init = h0 is not None
    cx = COL_LX // w
    cg = COL_LG // w
    vec = lambda b, k, r: (0, k)
    in_specs = [pl.BlockSpec((rows, w), lambda b, k, r: (b * nrb + r, cx + k)),
                pl.BlockSpec((rows, w), lambda b, k, r: (b * nrb + r, cg + k)),
                pl.BlockSpec((CONV_W, w), vec), pl.BlockSpec((1, w), vec),
                pl.BlockSpec((1, w, w), lambda b, k, r: (k, 0, 0)),
                pl.BlockSpec((1, w, w), lambda b, k, r: (k, 0, 0)),
                pl.BlockSpec((1, w), vec), pl.BlockSpec((1, w), vec), pl.BlockSpec((1, w), vec)]
    args = [proj, proj, conv_w, conv_b, w_a, w_x, b_a, b_x, lam]
    if has_init:
        in_specs += [pl.BlockSpec((1, CONV_W - 1, w), lambda b, k, r: (b, 0, k)),
                     pl.BlockSpec((1, 1, w), lambda b, k, r: (b, 0, k))]
        args += [conv_buf, h0]
    return pl.pallas_call(
        functools.partial(_lru_kernel, rows=rows, has_init=has_init),
        grid=(bsz, LRU_BLOCKS, nrb),
        in_specs=in_specs,
        out_specs=[pl.BlockSpec((rows, w), lambda b, k, r: (b * nrb + r, k)),
                   pl.BlockSpec((1, 1, w), lambda b, k, r: (b, 0, k))],
        out_shape=[jax.ShapeDtypeStruct((m, D_MODEL), BF16 if rows % 16 == 0 else F32),
                   jax.ShapeDtypeStruct((bsz, 1, D_MODEL), F32)],
        scratch_shapes=[pltpu.VMEM((1, w), F32), pltpu.VMEM((1, rows + 8, w), F32)],
        compiler_params=_cparams(("parallel", "parallel", "arbitrary")),
        name="lru",
    )(*args)


def _suffix_matrix():
    r = lax.broadcasted_iota(jnp.int32, (SB_BLOCK, 2 * SB_BLOCK), 0)
    c = lax.broadcasted_iota(jnp.int32, (SB_BLOCK, 2 * SB_BLOCK), 1)
    return jnp.where((r > c) | (c >= SB_BLOCK), 1.0, 0.0).astype(BF16)


def _sb_block(s, bias, u_ext, carry, mask):
    z = s * (SB_HEAD_DIM ** -0.5) + bias
    sp = _softplus(z)
    if mask is not None:
        sp = jnp.where(mask, sp, 0.0)
    suf = _split_dot(sp, u_ext, 2)
    w = jnp.exp(z - sp - suf[:, :SB_BLOCK] - carry)
    if mask is not None:
        w = jnp.where(mask, w, 0.0)
    return w, carry + suf[:, SB_BLOCK:]


def _sb_prompt_kernel(bias_ref, q_ref, k_ref, v_ref, u_ref, o_ref):
    h = pl.program_id(1)
    i = pl.program_id(2)
    tq = SB_BLOCK
    bias = bias_ref[h]
    u_ext = u_ref[...]
    qb = q_ref[...].astype(BF16)
    row = lax.broadcasted_iota(jnp.int32, (tq, SB_BLOCK), 0)
    col = lax.broadcasted_iota(jnp.int32, (tq, SB_BLOCK), 1)

    def block(jj, carry, acc, mask):
        off = pl.multiple_of(jj * SB_BLOCK, SB_BLOCK)
        kb = k_ref[pl.ds(off, SB_BLOCK), :].astype(BF16)
        vb = v_ref[pl.ds(off, SB_BLOCK), :].astype(BF16)
        w, carry = _sb_block(_dot_nt(qb, kb), bias, u_ext, carry, mask)
        return carry, acc + _dot(w.astype(BF16), vb)

    zeros = jnp.zeros((tq, SB_BLOCK), F32)
    carry, acc = block(i, zeros, zeros, col < row)

    def body(step, ca):
        return block(i - 1 - step, ca[0], ca[1], None)

    carry, acc = lax.fori_loop(0, i, body, (carry, acc))
    o_ref[...] = acc.astype(o_ref.dtype)


def _sb_prompt(proj, k_new, v_new, sb_bias, u_ext, bsz, t):
    m = bsz * t
    nq = t // SB_BLOCK
    cq = COL_Q // SB_HEAD_DIM
    grid_spec = pltpu.PrefetchScalarGridSpec(
        num_scalar_prefetch=1,
        grid=(bsz, SB_HEADS, nq),
        in_specs=[pl.BlockSpec((SB_BLOCK, SB_HEAD_DIM), lambda b, h, i, *_: (b * nq + i, cq + h)),
                  pl.BlockSpec((t, SB_HEAD_DIM), lambda b, h, i, *_: (b, h)),
                  pl.BlockSpec((t, SB_HEAD_DIM), lambda b, h, i, *_: (b, h)),
                  pl.BlockSpec((SB_BLOCK, 2 * SB_BLOCK), lambda b, h, i, *_: (0, 0))],
        out_specs=pl.BlockSpec((SB_BLOCK, SB_HEAD_DIM), lambda b, h, i, *_: (b * nq + i, h)),
    )
    return pl.pallas_call(
        _sb_prompt_kernel,
        grid_spec=grid_spec,
        out_shape=jax.ShapeDtypeStruct((m, D_MODEL), BF16),
        compiler_params=_cparams(("parallel", "parallel", "arbitrary")),
        name="sb_prompt",
    )(sb_bias, proj, k_new, v_new, u_ext)


def _sb_decode_kernel(pt_ref, bias_ref, q_ref, kn_ref, vn_ref, kc_ref, vc_ref, u_ref, o_ref, qf_ref, acc_ref, carry_ref, *, tq):
    j = pl.program_id(1)
    nj = pl.num_programs(1)
    nh = SB_HEADS
    u_ext = u_ref[...]
    rows = nh * tq
    bias = jnp.concatenate([jnp.zeros((tq, SB_BLOCK), F32) + bias_ref[h] for h in range(nh)], axis=0)

    def process(get_k, get_v, mask):
        qf = qf_ref[...]
        s = jnp.concatenate([_dot_nt(qf[h * tq:(h + 1) * tq].astype(BF16), get_k(h)) for h in range(nh)], axis=0)
        w, carry = _sb_block(s, bias, u_ext, carry_ref[...], mask)
        carry_ref[...] = carry
        pv = jnp.concatenate([_dot(w[h * tq:(h + 1) * tq].astype(BF16), get_v(h)) for h in range(nh)], axis=0)
        acc_ref[...] += pv

    @pl.when(j == 0)
    def _():
        q = q_ref[...]
        qf_ref[...] = jnp.concatenate([q[:, h * SB_HEAD_DIM:(h + 1) * SB_HEAD_DIM] for h in range(nh)], axis=0)
        acc_ref[...] = jnp.zeros_like(acc_ref)
        carry_ref[...] = jnp.zeros_like(carry_ref)
        pad = jnp.zeros((SB_BLOCK - tq, SB_HEAD_DIM), F32)
        kn = kn_ref[...]
        vn = vn_ref[...]
        qi = lax.broadcasted_iota(jnp.int32, (rows, SB_BLOCK), 0) % tq
        ki = lax.broadcasted_iota(jnp.int32, (rows, SB_BLOCK), 1)
        process(lambda h: jnp.concatenate([kn[:, h * SB_HEAD_DIM:(h + 1) * SB_HEAD_DIM], pad], axis=0).astype(BF16),
                lambda h: jnp.concatenate([vn[:, h * SB_HEAD_DIM:(h + 1) * SB_HEAD_DIM], pad], axis=0).astype(BF16),
                ki < qi)

    @pl.when(j > 0)
    def _():
        process(lambda h: kc_ref[0, pl.ds(h, SB_BLOCK, stride=nh), :].astype(BF16),
                lambda h: vc_ref[0, pl.ds(h, SB_BLOCK, stride=nh), :].astype(BF16),
                None)

    @pl.when(j == nj - 1)
    def _():
        acc = acc_ref[...]
        for h in range(nh):
            o_ref[:, h * SB_HEAD_DIM:(h + 1) * SB_HEAD_DIM] = acc[h * tq:(h + 1) * tq].astype(o_ref.dtype)


def _sb_decode(proj, k_new, v_new, cache_k, cache_v, page_table, sb_bias, u_ext, layer, bsz, tq):
    n_pages = page_table.shape[1]
    n_pool = cache_k.shape[0] // DEPTH
    page_rows = cache_k.shape[1]
    base = layer * n_pool

    def page_map(b, j, pt, bias):
        return (base + pt[b, n_pages - jnp.maximum(j, 1)], 0, 0)

    grid_spec = pltpu.PrefetchScalarGridSpec(
        num_scalar_prefetch=2,
        grid=(bsz, n_pages + 1),
        in_specs=[pl.BlockSpec((tq, D_MODEL), lambda b, j, *_: (b, COL_Q // D_MODEL)),
                  pl.BlockSpec((tq, D_MODEL), lambda b, j, *_: (b, 0)),
                  pl.BlockSpec((tq, D_MODEL), lambda b, j, *_: (b, 0)),
                  pl.BlockSpec((1, page_rows, SB_HEAD_DIM), page_map),
                  pl.BlockSpec((1, page_rows, SB_HEAD_DIM), page_map),
                  pl.BlockSpec((SB_BLOCK, 2 * SB_BLOCK), lambda b, j, *_: (0, 0))],
        out_specs=pl.BlockSpec((tq, D_MODEL), lambda b, j, *_: (b, 0)),
        scratch_shapes=[pltpu.VMEM((SB_HEADS * tq, SB_HEAD_DIM), F32),
                        pltpu.VMEM((SB_HEADS * tq, SB_HEAD_DIM), F32),
                        pltpu.VMEM((SB_HEADS * tq, SB_BLOCK), F32)],
    )
    return pl.pallas_call(
        functools.partial(_sb_decode_kernel, tq=tq),
        grid_spec=grid_spec,
        out_shape=jax.ShapeDtypeStruct((bsz * tq, D_MODEL), F32),
        compiler_params=_cparams(("parallel", "arbitrary")),
        name="sb_decode",
    )(page_table, sb_bias, proj, k_new, v_new, cache_k, cache_v, u_ext)


def _layer(x, mods, cfg, lw, u_ext, past=None, state=None):
    bsz, t, tm, bpb, rows = cfg
    sh1, sc1, g1, sh2, sc2, g2 = mods
    u = _norm_mod(x, lw["g_pre_mix"], sc1, sh1, tm, bpb)
    tn_main = 1024
    proj = _mm(u, lw["w_in"], 0, N_MAIN, tn_main, tm)
    k_new = _mm(u, lw["w_in"], COL_K, D_MODEL, tn_main, tm)
    v_new = _mm(u, lw["w_in"], COL_V, D_MODEL, tn_main, tm)
    dt_raw = _mm(u, lw["w_in"], COL_DT, LANES, LANES, tm)

    if state is None:
        ssd_buf = lru_buf = ssd_h0 = lru_h0 = None
    else:
        ssd_h0, ssd_buf, lru_h0, lru_buf = state
    y_ssd_raw, ssd_h = _ssd_scan(proj, dt_raw, bsz, t, rows, lw["ssd_conv_w"], lw["ssd_conv_b"], lw["ssd_dt_bias"],
                                 lw["ssd_a_log"], lw["ssd_d"], ssd_buf, ssd_h0)
    y_ssd = _ssd_gate(y_ssd_raw, proj, lw["ssd_norm_g"], tm)
    if past is None:
        y_sb = _sb_prompt(proj, k_new, v_new, lw["sb_bias"], u_ext, bsz, t)
    else:
        cache_k, cache_v, page_table, layer = past
        y_sb = _sb_decode(proj, k_new, v_new, cache_k, cache_v, page_table, lw["sb_bias"], u_ext, layer, bsz, t)
    y_lru, lru_h = _lru(proj, bsz, t, rows, lw["lru_conv_w"], lw["lru_conv_b"], lw["lru_w_a"], lw["lru_w_x"],
                        lw["lru_b_a"], lw["lru_b_x"], lw["lru_lambda"], lru_buf, lru_h0)
    merged = _merge(y_ssd, y_sb, y_lru, proj, lw["w_branch"], tm, 512)
    x = _mm_norm_res(merged, lw["w_out"], x, lw["g_post_mix"], g1, tm, D_MODEL, bpb)
    u2 = _norm_mod(x, lw["g_pre_ffn"], sc2, sh2, tm, bpb)
    hidden = _mm_glu(u2, lw["w_up"], tm, 512)
    x = _mm_norm_res(hidden, lw["w_down"], x, lw["g_post_ffn"], g2, tm, D_FF // 4, bpb)

    proj3 = proj.reshape(bsz, t, N_MAIN)
    ssd_conv = proj3[:, t - (CONV_W - 1):, COL_XBC:COL_XBC + SSD_XBC]
    lru_conv = proj3[:, t - (CONV_W - 1):, COL_LX:COL_LX + D_MODEL]
    outs = (k_new.reshape(bsz, t, SB_HEADS, SB_HEAD_DIM), v_new.reshape(bsz, t, SB_HEADS, SB_HEAD_DIM),
            ssd_h.reshape(bsz, SSD_HEADS, SSD_HEAD_DIM, SSD_STATE), ssd_conv,
            lru_h.reshape(bsz, D_MODEL), lru_conv)
    return x, outs


def _row2(v):
    return v.reshape(1, -1)


def kernel(x_prompt, x_sample, c_prompt, c_sample, cache_k, cache_v, page_table, state_ssd, state_ssd_conv, state_lru, state_lru_conv, w_ada, b_ada, g_pre_mix, g_post_mix, g_pre_ffn, g_post_ffn, w_in, ssd_conv_w, ssd_conv_b, ssd_dt_bias, ssd_a_log, ssd_d, ssd_norm_g, sb_bias, lru_conv_w, lru_conv_b, lru_w_a, lru_b_a, lru_w_x, lru_b_x, lru_lambda, w_branch, w_out, w_up, w_down):
    bp, tp, d = x_prompt.shape
    bs, ts, _ = x_sample.shape
    depth = w_in.shape[0]
    n_pool, page = cache_k.shape[1], cache_k.shape[2]
    assert page == SB_BLOCK and tp % 512 == 0 and ts % 8 == 0 and ts < SB_BLOCK

    off_xbc = D_MODEL
    off_dt = off_xbc + SSD_XBC
    off_q = off_dt + SSD_HEADS
    w_in_r = jnp.concatenate(
        [w_in[:, :, 0:D_MODEL], w_in[:, :, off_q:off_q + D_MODEL], w_in[:, :, off_xbc:off_dt],
         w_in[:, :, off_q + 3 * D_MODEL:], w_in[:, :, off_q + D_MODEL:off_q + 3 * D_MODEL],
         w_in[:, :, off_dt:off_q], jnp.zeros((depth, d, LANES - SSD_HEADS), w_in.dtype)], axis=2).astype(BF16)
    w_ada_b = w_ada.astype(BF16)
    w_branch_b = w_branch.astype(BF16)
    w_out_b = w_out.astype(BF16)
    w_up_b = w_up.astype(BF16)
    w_down_b = w_down.astype(BF16)
    lru_w_a_b = lru_w_a.astype(BF16)
    lru_w_x_b = lru_w_x.astype(BF16)
    u_ext = _suffix_matrix()
    cache_k2 = cache_k.reshape(depth * n_pool, page * SB_HEADS, SB_HEAD_DIM)
    cache_v2 = cache_v.reshape(depth * n_pool, page * SB_HEADS, SB_HEAD_DIM)

    xp = x_prompt.reshape(bp * tp, d)
    xs = x_sample.reshape(bs * ts, d)
    n_c = bp + bs
    n_c_pad = -(-n_c // 16) * 16
    c_all = jnp.concatenate([c_prompt, c_sample, jnp.zeros((n_c_pad - n_c, d), c_prompt.dtype)], axis=0)
    tm_p = 512
    cfg_p = (bp, tp, tm_p, tp // tm_p, 512)
    cfg_s = (bs, ts, bs * ts, 1, ts)

    per_layer = []
    for l in range(depth):
        lw = {
            "g_pre_mix": _row2(g_pre_mix[l]), "g_post_mix": _row2(g_post_mix[l]),
            "g_pre_ffn": _row2(g_pre_ffn[l]), "g_post_ffn": _row2(g_post_ffn[l]),
            "w_in": w_in_r[l], "ssd_conv_w": ssd_conv_w[l], "ssd_conv_b": _row2(ssd_conv_b[l]),
            "ssd_dt_bias": ssd_dt_bias[l], "ssd_a_log": ssd_a_log[l], "ssd_d": ssd_d[l],
            "ssd_norm_g": _row2(ssd_norm_g[l]), "sb_bias": sb_bias[l],
            "lru_conv_w": lru_conv_w[l], "lru_conv_b": _row2(lru_conv_b[l]),
            "lru_w_a": lru_w_a_b[l], "lru_w_x": lru_w_x_b[l], "lru_b_a": _row2(lru_b_a[l]), "lru_b_x": _row2(lru_b_x[l]),
            "lru_lambda": _row2(lru_lambda[l]), "w_branch": w_branch_b[l], "w_out": w_out_b[l],
            "w_up": w_up_b[l], "w_down": w_down_b[l],
        }
        mod = _mm(c_all, w_ada_b[l], 0, 6 * d, 1024, n_c_pad, bias=_row2(b_ada[l]), pre_silu=True)
        mods_p = tuple(mod[:bp, i * d:(i + 1) * d].reshape(bp, 1, d) for i in range(6))
        mods_s = tuple(jnp.repeat(mod[bp:n_c, i * d:(i + 1) * d], ts, axis=0).reshape(1, bs * ts, d) for i in range(6))
        xp, outs_p = _layer(xp, mods_p, cfg_p, lw, u_ext)
        state = (state_ssd[l].reshape(bs, SSD_PAIRS, 2 * SSD_HEAD_DIM, SSD_STATE), state_ssd_conv[l],
                 state_lru[l].reshape(bs, 1, d), state_lru_conv[l])
        xs, outs_s = _layer(xs, mods_s, cfg_s, lw, u_ext, past=(cache_k2, cache_v2, page_table, l), state=state)
        per_layer.append((outs_p, outs_s))

    def stack(group, idx):
        return jnp.stack([per_layer[l][group][idx] for l in range(depth)])

    return (xp.reshape(bp, tp, d), xs.reshape(bs, ts, d),
            stack(0, 0), stack(0, 1), stack(1, 0), stack(1, 1),
            stack(0, 2), stack(1, 2), stack(0, 3), stack(1, 3),
            stack(0, 4), stack(1, 4), stack(0, 5), stack(1, 5))
```

```python
import functools

import jax
import jax.numpy as jnp
from jax import lax
from jax.experimental import pallas as pl
from jax.experimental.pallas import tpu as pltpu

F32 = jnp.float32
BF16 = jnp.bfloat16

D_MODEL = 2048
DEPTH = 4
CONV_W = 4
RMS_EPS = 1e-6
SSD_HEAD_DIM = 64
SSD_HEADS = 32
SSD_GROUPS = 4
SSD_STATE = 128
SSD_CHUNK = 128
SSD_XBC = D_MODEL + 2 * SSD_GROUPS * SSD_STATE
SSD_PAIRS = SSD_HEADS // 2
SB_HEADS = 16
SB_HEAD_DIM = 128
SB_BLOCK = 128
LRU_BLOCKS = 8
LRU_BLOCK_DIM = 256
LRU_C = 8.0
N_BRANCH = 3
D_FF = 5632
LANES = 128

COL_Z = 0
COL_Q = COL_Z + D_MODEL
COL_XBC = COL_Q + D_MODEL
COL_LX = COL_XBC + SSD_XBC
COL_LG = COL_LX + D_MODEL
COL_GATE = COL_LG + D_MODEL
N_MAIN = COL_GATE + N_BRANCH * D_MODEL
COL_K = N_MAIN
COL_V = COL_K + D_MODEL
COL_DT = COL_V + D_MODEL
N_IN_PAD = COL_DT + LANES

LOG2E = 1.4426950408889634

VMEM_LIMIT = 48 * 1024 * 1024
TN_PROJ = 1024
TN_MERGE = 512
TN_GLU = 512
TK_DOWN = D_FF // 4
SB_GROUP = 4
SB_SUB = 4
SB_PAGES_PER_STEP = 4


def _cparams(sem):
    return pltpu.CompilerParams(dimension_semantics=sem, vmem_limit_bytes=VMEM_LIMIT)


def _dot(a, b):
    return jnp.dot(a, b, preferred_element_type=F32)


def _dot_nt(a, b):
    return lax.dot_general(a, b, (((1,), (1,)), ((), ())), preferred_element_type=F32)


def _dot_tn(a, b):
    return lax.dot_general(a, b, (((0,), (0,)), ((), ())), preferred_element_type=F32)


def _sigmoid(x):
    return 1.0 / (1.0 + jnp.exp(-x))


def _silu(x):
    return x * _sigmoid(x)


def _softplus(x):
    return jnp.maximum(x, 0.0) + jnp.log1p(jnp.exp(-jnp.abs(x)))


def _gelu_tanh(x):
    return 0.5 * x * (1.0 + jnp.tanh(0.7978845608028654 * (x + 0.044715 * (x * x * x))))


def _split_dot_left(w, x, terms):
    acc = None
    rem = x
    for t in range(terms):
        piece = rem.astype(BF16)
        part = _dot(w, piece)
        acc = part if acc is None else acc + part
        if t + 1 < terms:
            rem = rem - piece.astype(F32)
    return acc


def _mm_kernel(*refs, pre_silu, has_bias, n_in):
    a_ref, w_ref = refs[0], refs[1]
    a = a_ref[...]
    if pre_silu:
        a = _silu(a.astype(F32))
    acc = _dot(a.astype(BF16), w_ref[...])
    if has_bias:
        acc = acc + refs[2][...]
    for o_ref in refs[n_in:]:
        o_ref[...] = acc.astype(o_ref.dtype)


def _mm(a, w, layer, col0, ncols, tn, tm, bias=None, pre_silu=False, bf16_copy=False, slab=None):
    m, k = a.shape
    assert col0 % tn == 0 and ncols % tn == 0 and m % tm == 0
    cb = col0 // tn
    in_specs = [pl.BlockSpec((tm, k), lambda j, i: (i, 0)),
                pl.BlockSpec((None, k, tn), lambda j, i: (layer, 0, j + cb))]
    args = [a, w]
    if bias is not None:
        in_specs.append(pl.BlockSpec((1, tn), lambda j, i: (0, j + cb)))
        args.append(bias)
    aliases = {}
    rb, out_rows = 0, m
    if slab is not None:
        dest, n_slabs, index = slab
        rb, out_rows = index * (m // tm), n_slabs * m
        if dest is not None:
            aliases = {len(args): 0}
            in_specs.append(pl.BlockSpec(memory_space=pl.ANY))
            args.append(dest)
    out_specs = [pl.BlockSpec((tm, tn), lambda j, i: (i + rb, j))]
    out_shape = [jax.ShapeDtypeStruct((out_rows, ncols), F32)]
    if bf16_copy:
        out_specs.append(pl.BlockSpec((tm, tn), lambda j, i: (i, j)))
        out_shape.append(jax.ShapeDtypeStruct((m, ncols), BF16))
    outs = pl.pallas_call(
        functools.partial(_mm_kernel, pre_silu=pre_silu, has_bias=bias is not None, n_in=len(args)),
        grid=(ncols // tn, m // tm),
        in_specs=in_specs,
        out_specs=out_specs,
        out_shape=out_shape,
        input_output_aliases=aliases,
        compiler_params=_cparams(("parallel", "parallel")),
        name="mm",
    )(*args)
    return outs if bf16_copy else outs[0]


def _mod_spec(mod, bpb):
    _, r, d = mod.shape
    return pl.BlockSpec((1, r, d), lambda i, *_: (i // bpb, 0, 0))


def _norm_mod_kernel(x_ref, g_ref, sc_ref, sh_ref, o_ref):
    x = x_ref[...]
    ms = jnp.mean(x * x, axis=-1, keepdims=True)
    y = x * lax.rsqrt(ms + RMS_EPS) * g_ref[...]
    o_ref[...] = (y * (1.0 + sc_ref[0]) + sh_ref[0]).astype(o_ref.dtype)


def _norm_mod(x, g, scale, shift, tm, bpb):
    m, d = x.shape
    return pl.pallas_call(
        _norm_mod_kernel,
        grid=(m // tm,),
        in_specs=[pl.BlockSpec((tm, d), lambda i: (i, 0)),
                  pl.BlockSpec((1, d), lambda i: (0, 0)),
                  _mod_spec(scale, bpb), _mod_spec(shift, bpb)],
        out_specs=pl.BlockSpec((tm, d), lambda i: (i, 0)),
        out_shape=jax.ShapeDtypeStruct((m, d), BF16),
        compiler_params=_cparams(("parallel",)),
        name="norm_mod",
    )(x, g, scale, shift)


def _mm_norm_res_kernel(a_ref, w_ref, x_ref, g_ref, gm_ref, o_ref, acc_ref, *, nk):
    k = pl.program_id(1)

    @pl.when(k == 0)
    def _():
        acc_ref[...] = jnp.zeros_like(acc_ref)

    acc_ref[...] += _dot(a_ref[...], w_ref[...])

    @pl.when(k == nk - 1)
    def _():
        mix = acc_ref[...]
        ms = jnp.mean(mix * mix, axis=-1, keepdims=True)
        o_ref[...] = x_ref[...] + gm_ref[0] * (mix * lax.rsqrt(ms + RMS_EPS) * g_ref[...])


def _mm_norm_res(a, w, layer, x, g, gmod, tm, tk, bpb):
    m, kdim = a.shape
    d = w.shape[2]
    nk = kdim // tk
    return pl.pallas_call(
        functools.partial(_mm_norm_res_kernel, nk=nk),
        grid=(m // tm, nk),
        in_specs=[pl.BlockSpec((tm, tk), lambda i, k: (i, k)),
                  pl.BlockSpec((None, tk, d), lambda i, k: (layer, k, 0)),
                  pl.BlockSpec((tm, d), lambda i, k: (i, 0)),
                  pl.BlockSpec((1, d), lambda i, k: (0, 0)),
                  _mod_spec(gmod, bpb)],
        out_specs=pl.BlockSpec((tm, d), lambda i, k: (i, 0)),
        out_shape=jax.ShapeDtypeStruct((m, d), F32),
        scratch_shapes=[pltpu.VMEM((tm, d), F32)],
        compiler_params=_cparams(("parallel", "arbitrary")),
        name="mm_norm_res",
    )(a, w, x, g, gmod)


def _mm_glu_kernel(a_ref, wg_ref, wu_ref, o_ref):
    a = a_ref[...]
    gate = _dot(a, wg_ref[...])
    up = _dot(a, wu_ref[...])
    o_ref[...] = (_silu(gate) * up).astype(o_ref.dtype)


def _mm_glu(a, w_up, layer, tm, tn):
    m, k = a.shape
    nb = D_FF // tn
    return pl.pallas_call(
        _mm_glu_kernel,
        grid=(nb, m // tm),
        in_specs=[pl.BlockSpec((tm, k), lambda j, i: (i, 0)),
                  pl.BlockSpec((None, k, tn), lambda j, i: (layer, 0, j)),
                  pl.BlockSpec((None, k, tn), lambda j, i: (layer, 0, j + nb))],
        out_specs=pl.BlockSpec((tm, tn), lambda j, i: (i, j)),
        out_shape=jax.ShapeDtypeStruct((m, D_FF), BF16),
        compiler_params=_cparams(("parallel", "parallel")),
        name="mm_glu",
    )(a, w_up, w_up)


def _merge_kernel(y0_ref, y1_ref, y2_ref, w_ref, g0_ref, g1_ref, g2_ref, o_ref):
    acc = _sigmoid(g0_ref[...]) * _dot(y0_ref[...].astype(BF16), w_ref[0])
    acc = acc + _sigmoid(g1_ref[...]) * _dot(y1_ref[...].astype(BF16), w_ref[1])
    acc = acc + _sigmoid(g2_ref[...]) * _dot(y2_ref[...].astype(BF16), w_ref[2])
    o_ref[...] = acc.astype(o_ref.dtype)


def _merge(y_ssd, y_sb, y_lru, proj, w_branch, layer, tm, tn):
    m, d = y_ssd.shape
    gb = COL_GATE // tn
    nb = d // tn
    y_spec = pl.BlockSpec((tm, d), lambda i, j: (i, 0))
    return pl.pallas_call(
        _merge_kernel,
        grid=(m // tm, nb),
        in_specs=[y_spec, y_spec, y_spec,
                  pl.BlockSpec((None, N_BRANCH, d, tn), lambda i, j: (layer, 0, 0, j)),
                  pl.BlockSpec((tm, tn), lambda i, j: (i, gb + j)),
                  pl.BlockSpec((tm, tn), lambda i, j: (i, gb + nb + j)),
                  pl.BlockSpec((tm, tn), lambda i, j: (i, gb + 2 * nb + j))],
        out_specs=pl.BlockSpec((tm, tn), lambda i, j: (i, j)),
        out_shape=jax.ShapeDtypeStruct((m, d), BF16),
        compiler_params=_cparams(("parallel", "parallel")),
        name="merge",
    )(y_ssd, y_sb, y_lru, w_branch, proj, proj, proj)


def _conv_from_scratch(xp_ref, s, raw, w, b, rows):
    xp_ref[s, 8:8 + rows, :] = raw
    y = b + w[3:4] * raw
    for j in range(CONV_W - 1):
        y = y + w[j:j + 1] * xp_ref[s, 5 + j:5 + j + rows, :]
    if rows >= 8:
        xp_ref[s, 0:8, :] = raw[rows - 8:rows]
    return y


def _ssd_kernel(*refs, rows, has_init):
    if has_init:
        (dtb_ref, alog_ref, dsk_ref, x_ref, bm_ref, cm_ref, dt_ref, cwx_ref, cwb_ref, cwc_ref, cbx_ref, cbb_ref, cbc_ref,
         bufx_ref, bufb_ref, bufc_ref, h0_ref, y_ref, hout_ref, h_ref, xp_ref) = refs
    else:
        (dtb_ref, alog_ref, dsk_ref, x_ref, bm_ref, cm_ref, dt_ref, cwx_ref, cwb_ref, cwc_ref, cbx_ref, cbb_ref, cbc_ref,
         y_ref, hout_ref, h_ref, xp_ref) = refs
    p = pl.program_id(1)
    r = pl.program_id(2)
    nr = pl.num_programs(2)
    ln = SSD_CHUNK

    @pl.when(r == 0)
    def _():
        if has_init:
            h_ref[...] = h0_ref[0, 0]
            xp_ref[0, 5:8, :] = bufx_ref[0]
            xp_ref[1, 5:8, :] = bufb_ref[0]
            xp_ref[2, 5:8, :] = bufc_ref[0]
        else:
            h_ref[...] = jnp.zeros_like(h_ref)
            xp_ref[:, 0:8, :] = jnp.zeros((3, 8, LANES), F32)

    xs_all = _silu(_conv_from_scratch(xp_ref, 0, x_ref[...], cwx_ref[...], cbx_ref[...], rows))
    bm_all = _silu(_conv_from_scratch(xp_ref, 1, bm_ref[...], cwb_ref[...], cbb_ref[...], rows))
    cm_all = _silu(_conv_from_scratch(xp_ref, 2, cm_ref[...], cwc_ref[...], cbc_ref[...], rows))
    dt_all = dt_ref[...]

    lane = lax.broadcasted_iota(jnp.int32, (ln, LANES), 1)
    row = lax.broadcasted_iota(jnp.int32, (ln, LANES), 0)
    first_half = lane < SSD_HEAD_DIM
    tril = row >= lane
    eye = row == lane
    tri_bf = jnp.where(tril, 1.0, 0.0).astype(BF16)

    h0i = 2 * p
    h1i = 2 * p + 1
    a0 = -jnp.exp(jnp.zeros((1, LANES), F32) + alog_ref[h0i])
    a1 = -jnp.exp(jnp.zeros((1, LANES), F32) + alog_ref[h1i])
    dskip = jnp.where(first_half[0:1], dsk_ref[h0i], dsk_ref[h1i])

    n_chunks = max(rows // ln, 1)
    for c in range(n_chunks):
        if rows >= ln:
            xs = xs_all[c * ln:(c + 1) * ln]
            bm = bm_all[c * ln:(c + 1) * ln]
            cm = cm_all[c * ln:(c + 1) * ln]
            dtr = dt_all[c * ln:(c + 1) * ln]
            valid = None
        else:
            pad = jnp.zeros((ln - rows, LANES), F32)
            xs = jnp.concatenate([xs_all, pad], axis=0)
            bm = jnp.concatenate([bm_all, pad], axis=0)
            cm = jnp.concatenate([cm_all, pad], axis=0)
            dtr = jnp.concatenate([dt_all, pad], axis=0)
            valid = row < rows

        dtr0 = jnp.sum(jnp.where(lane == h0i, dtr, 0.0), axis=1, keepdims=True)
        dtr1 = jnp.sum(jnp.where(lane == h1i, dtr, 0.0), axis=1, keepdims=True)
        dt0 = _softplus(jnp.broadcast_to(dtr0, (ln, LANES)) + dtb_ref[h0i])
        dt1 = _softplus(jnp.broadcast_to(dtr1, (ln, LANES)) + dtb_ref[h1i])
        if valid is not None:
            dt0 = jnp.where(valid, dt0, 0.0)
            dt1 = jnp.where(valid, dt1, 0.0)
        la = jnp.concatenate([dt0 * a0, dt1 * a1], axis=1)
        acum = _split_dot_left(tri_bf, la, 3)
        ac0 = acum[:, :LANES]
        ac1 = acum[:, LANES:]
        acp = jnp.where(first_half, ac0, ac1)
        dtp = jnp.where(first_half, dt0, dt1)
        last0 = ac0[ln - 1:ln]
        last1 = ac1[ln - 1:ln]
        lastp = acp[ln - 1:ln]

        cb = _dot_nt(cm.astype(BF16), bm.astype(BF16))
        xs_bf = xs.astype(BF16)

        def scores(ac, dt):
            arow = jnp.sum(jnp.where(eye, ac, 0.0), axis=0, keepdims=True)
            dtrow = jnp.sum(jnp.where(eye, dt, 0.0), axis=0, keepdims=True)
            decay = jnp.exp(jnp.where(tril, ac - arow, -jnp.inf))
            return (cb * decay * dtrow).astype(BF16)

        y_diag = jnp.where(first_half, _dot(scores(ac0, dt0), xs_bf), _dot(scores(ac1, dt1), xs_bf))
        h_prev = h_ref[...]
        y_off = jnp.exp(acp) * _dot_nt(cm.astype(BF16), h_prev.astype(BF16))
        y = y_diag + y_off + dskip * xs
        y_ref[c * ln:c * ln + min(rows, ln), :] = y[:min(rows, ln)]

        w_state = jnp.exp(lastp - acp) * dtp
        states = _dot_tn((xs * w_state).astype(BF16), bm.astype(BF16))
        cdec = jnp.where(row < SSD_HEAD_DIM, jnp.broadcast_to(last0, (ln, LANES)), jnp.broadcast_to(last1, (ln, LANES)))
        h_ref[...] = jnp.exp(cdec) * h_prev + states

    @pl.when(r == nr - 1)
    def _():
        hout_ref[0, 0] = h_ref[...]


def _ssd_scan(proj, dt_raw, bsz, t, rows, conv_w, conv_b, dt_bias, a_log, d_skip, conv_buf=None, h0=None):
    m = bsz * t
    nrb = t // rows
    has_init = h0 is not None
    cx = COL_XBC // LANES
    cbm = cx + D_MODEL // LANES
    ccm = cbm + SSD_GROUPS
    hpg = SSD_PAIRS // SSD_GROUPS

    def rowmap(off_fn):
        return lambda b, p, r, *_: (b * nrb + r, off_fn(p))

    def wmap(off_fn):
        return lambda b, p, r, *_: (0, off_fn(p))

    def bufmap(off_fn):
        return lambda b, p, r, *_: (b, 0, off_fn(p))

    fx = lambda p: p
    fb = lambda p: D_MODEL // LANES + p // hpg
    fc = lambda p: D_MODEL // LANES + SSD_GROUPS + p // hpg
    in_specs = [
        pl.BlockSpec((rows, LANES), rowmap(lambda p: cx + p)),
        pl.BlockSpec((rows, LANES), rowmap(lambda p: cbm + p // hpg)),
        pl.BlockSpec((rows, LANES), rowmap(lambda p: ccm + p // hpg)),
        pl.BlockSpec((rows, LANES), rowmap(lambda p: 0)),
        pl.BlockSpec((CONV_W, LANES), wmap(fx)), pl.BlockSpec((CONV_W, LANES), wmap(fb)), pl.BlockSpec((CONV_W, LANES), wmap(fc)),
        pl.BlockSpec((1, LANES), wmap(fx)), pl.BlockSpec((1, LANES), wmap(fb)), pl.BlockSpec((1, LANES), wmap(fc)),
    ]
    args = [proj, proj, proj, dt_raw, conv_w, conv_w, conv_w, conv_b, conv_b, conv_b]
    if has_init:
        in_specs += [pl.BlockSpec((1, CONV_W - 1, LANES), bufmap(fx)),
                     pl.BlockSpec((1, CONV_W - 1, LANES), bufmap(fb)),
                     pl.BlockSpec((1, CONV_W - 1, LANES), bufmap(fc)),
                     pl.BlockSpec((1, 1, LANES, SSD_STATE), lambda b, p, r, *_: (b, p, 0, 0))]
        args += [conv_buf, conv_buf, conv_buf, h0]
    grid_spec = pltpu.PrefetchScalarGridSpec(
        num_scalar_prefetch=3,
        grid=(bsz, SSD_PAIRS, nrb),
        in_specs=in_specs,
        out_specs=[pl.BlockSpec((rows, LANES), rowmap(lambda p: p)),
                   pl.BlockSpec((1, 1, LANES, SSD_STATE), lambda b, p, r, *_: (b, p, 0, 0))],
        scratch_shapes=[pltpu.VMEM((LANES, SSD_STATE), F32), pltpu.VMEM((3, rows + 8, LANES), F32)],
    )
    return pl.pallas_call(
        functools.partial(_ssd_kernel, rows=rows, has_init=has_init),
        grid_spec=grid_spec,
        out_shape=[jax.ShapeDtypeStruct((m, D_MODEL), F32),
                   jax.ShapeDtypeStruct((bsz, SSD_PAIRS, LANES, SSD_STATE), F32)],
        compiler_params=_cparams(("parallel", "parallel", "arbitrary")),
        name="ssd_scan",
    )(dt_bias, a_log, d_skip, *args)


def _ssd_gate_kernel(y_ref, z_ref, g_ref, o_ref):
    y = y_ref[...] * _silu(z_ref[...])
    ms = jnp.mean(y * y, axis=-1, keepdims=True)
    o_ref[...] = (y * lax.rsqrt(ms + RMS_EPS) * g_ref[...]).astype(o_ref.dtype)


def _ssd_gate(y, proj, norm_g, tm):
    m, d = y.shape
    return pl.pallas_call(
        _ssd_gate_kernel,
        grid=(m // tm,),
        in_specs=[pl.BlockSpec((tm, d), lambda i: (i, 0)),
                  pl.BlockSpec((tm, d), lambda i: (i, COL_Z // D_MODEL)),
                  pl.BlockSpec((1, d), lambda i: (0, 0))],
        out_specs=pl.BlockSpec((tm, d), lambda i: (i, 0)),
        out_shape=jax.ShapeDtypeStruct((m, d), BF16),
        compiler_params=_cparams(("parallel",)),
        name="ssd_gate",
    )(y, proj, norm_g)


def _lru_kernel(*refs, rows, has_init):
    if has_init:
        (x_ref, g_ref, cw_ref, cb_ref, wa_ref, wx_ref, ba_ref, bx_ref, lam_ref, buf_ref, h0_ref,
         y_ref, hout_ref, h_ref, xp_ref) = refs
    else:
        (x_ref, g_ref, cw_ref, cb_ref, wa_ref, wx_ref, ba_ref, bx_ref, lam_ref,
         y_ref, hout_ref, h_ref, xp_ref) = refs
    r = pl.program_id(2)
    nr = pl.num_programs(2)
    w = LRU_BLOCK_DIM

    @pl.when(r == 0)
    def _():
        if has_init:
            h_ref[...] = h0_ref[0]
            xp_ref[0, 5:8, :] = buf_ref[0]
        else:
            h_ref[...] = jnp.zeros_like(h_ref)
            xp_ref[0, 0:8, :] = jnp.zeros((8, w), F32)

    xc = _conv_from_scratch(xp_ref, 0, x_ref[...], cw_ref[...], cb_ref[...], rows)
    xb = xc.astype(BF16)
    rg = _sigmoid(_dot(xb, wa_ref[0]) + ba_ref[...])
    ig = _sigmoid(_dot(xb, wx_ref[0]) + bx_ref[...])
    log_a = -LRU_C * rg * _softplus(-lam_ref[...])
    a = jnp.exp(log_a)
    b = jnp.sqrt(-jnp.tanh(log_a) * (a * a + 1.0)) * (ig * xc)

    row = lax.broadcasted_iota(jnp.int32, (rows, w), 0)
    d = 1
    while d < rows:
        keep = row >= d
        a_sh = jnp.where(keep, pltpu.roll(a, d, 0), 1.0)
        b_sh = jnp.where(keep, pltpu.roll(b, d, 0), 0.0)
        b = a * b_sh + b
        a = a * a_sh
        d *= 2
    h = a * h_ref[...] + b
    h_ref[...] = h[rows - 1:rows]
    y_ref[...] = (_gelu_tanh(g_ref[...]) * h).astype(y_ref.dtype)

    @pl.when(r == nr - 1)
    def _():
        hout_ref[0] = h[rows - 1:rows]


def _lru(proj, bsz, t, rows, conv_w, conv_b, w_a, w_x, layer, b_a, b_x, lam, conv_buf=None, h0=None):
    m = bsz * t
    nrb = t // rows
    w = LRU_BLOCK_DIM
    has_init = h0 is not None
    cx = COL_LX // w
    cg = COL_LG // w
    vec = lambda b, k, r: (0, k)
    in_specs = [pl.BlockSpec((rows, w), lambda b, k, r: (b * nrb + r, cx + k)),
                pl.BlockSpec((rows, w), lambda b, k, r: (b * nrb + r, cg + k)),
                pl.BlockSpec((CONV_W, w), vec), pl.BlockSpec((1, w), vec),
                pl.BlockSpec((None, 1, w, w), lambda b, k, r: (layer, k, 0, 0)),
                pl.BlockSpec((None, 1, w, w), lambda b, k, r: (layer, k, 0, 0)),
                pl.BlockSpec((1, w), vec), pl.BlockSpec((1, w), vec), pl.BlockSpec((1, w), vec)]
    args = [proj, proj, conv_w, conv_b, w_a, w_x, b_a, b_x, lam]
    if has_init:
        in_specs += [pl.BlockSpec((1, CONV_W - 1, w), lambda b, k, r: (b, 0, k)),
                     pl.BlockSpec((1, 1, w), lambda b, k, r: (b, 0, k))]
        args += [conv_buf, h0]
    return pl.pallas_call(
        functools.partial(_lru_kernel, rows=rows, has_init=has_init),
        grid=(bsz, LRU_BLOCKS, nrb),
        in_specs=in_specs,
        out_specs=[pl.BlockSpec((rows, w), lambda b, k, r: (b * nrb + r, k)),
                   pl.BlockSpec((1, 1, w), lambda b, k, r: (b, 0, k))],
        out_shape=[jax.ShapeDtypeStruct((m, D_MODEL), BF16 if rows % 16 == 0 else F32),
                   jax.ShapeDtypeStruct((bsz, 1, D_MODEL), F32)],
        scratch_shapes=[pltpu.VMEM((1, w), F32), pltpu.VMEM((1, rows + 8, w), F32)],
        compiler_params=_cparams(("parallel", "parallel", "arbitrary")),
        name="lru",
    )(*args)


def _suffix_matrix():
    r = lax.broadcasted_iota(jnp.int32, (2 * SB_BLOCK, 2 * SB_BLOCK), 0) % SB_BLOCK
    c = lax.broadcasted_iota(jnp.int32, (2 * SB_BLOCK, 2 * SB_BLOCK), 1)
    return jnp.where((r >= c) | (c >= SB_BLOCK), 1.0, 0.0).astype(BF16)


def _sb_scores(s, scale2, bias2, mask):
    z2 = s * scale2 + bias2
    sp = jnp.maximum(z2, 0.0) + jnp.log2(1.0 + jnp.exp2(-jnp.abs(z2)))
    if mask is not None:
        sp = jnp.where(mask, sp, 0.0)
    return z2, sp


def _sb_suffix(sp, u_ext):
    hi = sp.astype(BF16)
    lo = (sp - hi.astype(F32)).astype(BF16)
    return _dot(jnp.concatenate([hi, lo], axis=1), u_ext)


def _sb_prompt_kernel(bias_ref, q_ref, k_ref, v_ref, u_ref, o_ref, acc_ref, carry_ref, *, g_heads, n_sub):
    hg = pl.program_id(1)
    i = pl.program_id(2)
    tq = SB_BLOCK
    wide = n_sub * SB_BLOCK
    u_ext = u_ref[...]
    scale2 = (SB_HEAD_DIM ** -0.5) * LOG2E
    qbs = [q_ref[:, g * SB_HEAD_DIM:(g + 1) * SB_HEAD_DIM].astype(BF16) for g in range(g_heads)]
    biases = [bias_ref[hg * g_heads + g] * LOG2E for g in range(g_heads)]
    row = lax.broadcasted_iota(jnp.int32, (tq, wide), 0)
    col = lax.broadcasted_iota(jnp.int32, (tq, wide), 1)

    def stripe(st, masked):
        rows = pl.ds(pl.multiple_of(st * wide, wide), wide)
        mask = (st * wide + col) < (i * tq + row) if masked else None
        zs = [_dot_nt(qbs[g], k_ref[rows, g * SB_HEAD_DIM:(g + 1) * SB_HEAD_DIM]) for g in range(g_heads)]
        z2s, sufs = [], []
        for g in range(g_heads):
            z2, sp = _sb_scores(zs[g], scale2, biases[g], mask)
            z2s.append(z2)
            sufs.append([_sb_suffix(sp[:, s * SB_BLOCK:(s + 1) * SB_BLOCK], u_ext) for s in range(n_sub)])
        for g in range(g_heads):
            carry = carry_ref[g]
            parts = [None] * n_sub
            for s in reversed(range(n_sub)):
                suf = sufs[g][s]
                parts[s] = jnp.exp2(z2s[g][:, s * SB_BLOCK:(s + 1) * SB_BLOCK] - suf[:, :SB_BLOCK] - carry)
                carry = carry + suf[:, SB_BLOCK:]
            carry_ref[g] = carry
            w = jnp.concatenate(parts, axis=1)
            if masked:
                w = jnp.where(mask, w, 0.0)
            acc_ref[g] += _dot(w.astype(BF16), v_ref[rows, g * SB_HEAD_DIM:(g + 1) * SB_HEAD_DIM])

    acc_ref[...] = jnp.zeros_like(acc_ref)
    carry_ref[...] = jnp.zeros_like(carry_ref)
    diag = i // n_sub
    stripe(diag, True)

    def body(step, c):
        stripe(diag - 1 - step, False)
        return c

    lax.fori_loop(0, diag, body, 0)
    for g in range(g_heads):
        o_ref[:, g * SB_HEAD_DIM:(g + 1) * SB_HEAD_DIM] = acc_ref[g].astype(o_ref.dtype)


def _sb_prompt(proj, k_bf, v_bf, sb_bias, u_ext, bsz, t):
    m = bsz * t
    nq = t // SB_BLOCK
    w = SB_GROUP * SB_HEAD_DIM
    cq = COL_Q // w
    assert t % (SB_SUB * SB_BLOCK) == 0
    grid_spec = pltpu.PrefetchScalarGridSpec(
        num_scalar_prefetch=1,
        grid=(bsz, SB_HEADS // SB_GROUP, nq),
        in_specs=[pl.BlockSpec((SB_BLOCK, w), lambda b, h, i, *_: (b * nq + i, cq + h)),
                  pl.BlockSpec((t, w), lambda b, h, i, *_: (b, h)),
                  pl.BlockSpec((t, w), lambda b, h, i, *_: (b, h)),
                  pl.BlockSpec((2 * SB_BLOCK, 2 * SB_BLOCK), lambda b, h, i, *_: (0, 0))],
        out_specs=pl.BlockSpec((SB_BLOCK, w), lambda b, h, i, *_: (b * nq + i, h)),
        scratch_shapes=[pltpu.VMEM((SB_GROUP, SB_BLOCK, SB_HEAD_DIM), F32),
                        pltpu.VMEM((SB_GROUP, SB_BLOCK, SB_BLOCK), F32)],
    )
    return pl.pallas_call(
        functools.partial(_sb_prompt_kernel, g_heads=SB_GROUP, n_sub=SB_SUB),
        grid_spec=grid_spec,
        out_shape=jax.ShapeDtypeStruct((m, D_MODEL), BF16),
        compiler_params=_cparams(("parallel", "parallel", "arbitrary")),
        name="sb_prompt",
    )(sb_bias, proj, k_bf, v_bf, u_ext)


def _sb_decode_kernel(pt_ref, bias_ref, q_ref, kn_ref, vn_ref, *rest, tq, n_pg):
    kc_refs = rest[:n_pg]
    vc_refs = rest[n_pg:2 * n_pg]
    u_ref, o_ref, qf_ref, acc_ref, carry_ref = rest[2 * n_pg:]
    j = pl.program_id(1)
    nj = pl.num_programs(1)
    nh = SB_HEADS
    u_ext = u_ref[...]
    rows = nh * tq
    scale2 = (SB_HEAD_DIM ** -0.5) * LOG2E
    bias2 = jnp.concatenate([jnp.zeros((tq, SB_BLOCK), F32) + bias_ref[h] * LOG2E for h in range(nh)], axis=0)

    def process(k_getters, v_getters, mask):
        qf = qf_ref[...]
        qhs = [qf[h * tq:(h + 1) * tq].astype(BF16) for h in range(nh)]
        zs = [jnp.concatenate([_dot_nt(qhs[h], get_k(h)) for h in range(nh)], axis=0) for get_k in k_getters]
        z2s, sufs = [], []
        for z in zs:
            z2, sp = _sb_scores(z, scale2, bias2, mask)
            z2s.append(z2)
            sufs.append(_sb_suffix(sp, u_ext))
        carry = carry_ref[...]
        acc = acc_ref[...]
        for z2, suf, get_v in zip(z2s, sufs, v_getters):
            w = jnp.exp2(z2 - suf[:, :SB_BLOCK] - carry)
            carry = carry + suf[:, SB_BLOCK:]
            if mask is not None:
                w = jnp.where(mask, w, 0.0)
            acc = acc + jnp.concatenate([_dot(w[h * tq:(h + 1) * tq].astype(BF16), get_v(h)) for h in range(nh)], axis=0)
        carry_ref[...] = carry
        acc_ref[...] = acc

    def page_tile(ref, h):
        return ref[0, pl.ds(h, SB_BLOCK, stride=nh), :].astype(BF16)

    @pl.when(j == 0)
    def _():
        q = q_ref[...]
        qf_ref[...] = jnp.concatenate([q[:, h * SB_HEAD_DIM:(h + 1) * SB_HEAD_DIM] for h in range(nh)], axis=0)
        acc_ref[...] = jnp.zeros_like(acc_ref)
        carry_ref[...] = jnp.zeros_like(carry_ref)
        pad = jnp.zeros((SB_BLOCK - tq, SB_HEAD_DIM), F32)
        kn = kn_ref[...]
        vn = vn_ref[...]
        qi = lax.broadcasted_iota(jnp.int32, (rows, SB_BLOCK), 0) % tq
        ki = lax.broadcasted_iota(jnp.int32, (rows, SB_BLOCK), 1)
        process([lambda h: jnp.concatenate([kn[:, h * SB_HEAD_DIM:(h + 1) * SB_HEAD_DIM], pad], axis=0).astype(BF16)],
                [lambda h: jnp.concatenate([vn[:, h * SB_HEAD_DIM:(h + 1) * SB_HEAD_DIM], pad], axis=0).astype(BF16)],
                ki < qi)

    @pl.when(j > 0)
    def _():
        process([functools.partial(page_tile, r) for r in kc_refs],
                [functools.partial(page_tile, r) for r in vc_refs],
                None)

    @pl.when(j == nj - 1)
    def _():
        acc = acc_ref[...]
        for h in range(nh):
            o_ref[:, h * SB_HEAD_DIM:(h + 1) * SB_HEAD_DIM] = acc[h * tq:(h + 1) * tq].astype(o_ref.dtype)


def _sb_decode(proj, k_new, v_new, cache_k, cache_v, page_table, sb_bias, u_ext, layer, bsz, tq):
    n_pages = page_table.shape[1]
    n_pool = cache_k.shape[0] // DEPTH
    page_rows = cache_k.shape[1]
    base = layer * n_pool
    n_pg = SB_PAGES_PER_STEP
    assert n_pages % n_pg == 0

    def page_map(r):
        return lambda b, j, pt, bias: (base + pt[b, n_pages - n_pg * jnp.maximum(j, 1) + (n_pg - 1 - r)], 0, 0)

    page_specs = [pl.BlockSpec((1, page_rows, SB_HEAD_DIM), page_map(r)) for r in range(n_pg)]
    grid_spec = pltpu.PrefetchScalarGridSpec(
        num_scalar_prefetch=2,
        grid=(bsz, n_pages // n_pg + 1),
        in_specs=[pl.BlockSpec((tq, D_MODEL), lambda b, j, *_: (b, COL_Q // D_MODEL)),
                  pl.BlockSpec((tq, D_MODEL), lambda b, j, *_: (b, 0)),
                  pl.BlockSpec((tq, D_MODEL), lambda b, j, *_: (b, 0))]
                 + page_specs + page_specs
                 + [pl.BlockSpec((2 * SB_BLOCK, 2 * SB_BLOCK), lambda b, j, *_: (0, 0))],
        out_specs=pl.BlockSpec((tq, D_MODEL), lambda b, j, *_: (b, 0)),
        scratch_shapes=[pltpu.VMEM((SB_HEADS * tq, SB_HEAD_DIM), F32),
                        pltpu.VMEM((SB_HEADS * tq, SB_HEAD_DIM), F32),
                        pltpu.VMEM((SB_HEADS * tq, SB_BLOCK), F32)],
    )
    return pl.pallas_call(
        functools.partial(_sb_decode_kernel, tq=tq, n_pg=n_pg),
        grid_spec=grid_spec,
        out_shape=jax.ShapeDtypeStruct((bsz * tq, D_MODEL), F32),
        compiler_params=_cparams(("parallel", "arbitrary")),
        name="sb_decode",
    )(page_table, sb_bias, proj, k_new, v_new, *([cache_k] * n_pg), *([cache_v] * n_pg), u_ext)


def _layer(x, mods, cfg, lw, layer, u_ext, kv_slabs=None, past=None, state=None):
    bsz, t, tm, bpb, rows = cfg
    sh1, sc1, g1, sh2, sc2, g2 = mods
    w_in = lw["w_in"]
    u = _norm_mod(x, lw["g_pre_mix"], sc1, sh1, tm, bpb)
    proj = _mm(u, w_in, layer, 0, N_MAIN, TN_PROJ, tm)
    dt_raw = _mm(u, w_in, layer, COL_DT, LANES, LANES, tm)
    if past is None:
        k_dest, v_dest = kv_slabs if kv_slabs is not None else (None, None)
        k_new, k_bf = _mm(u, w_in, layer, COL_K, D_MODEL, TN_PROJ, tm, bf16_copy=True, slab=(k_dest, DEPTH, layer))
        v_new, v_bf = _mm(u, w_in, layer, COL_V, D_MODEL, TN_PROJ, tm, bf16_copy=True, slab=(v_dest, DEPTH, layer))
        y_sb = _sb_prompt(proj, k_bf, v_bf, lw["sb_bias"], u_ext, bsz, t)
    else:
        cache_k, cache_v, page_table = past
        k_new = _mm(u, w_in, layer, COL_K, D_MODEL, TN_PROJ, tm)
        v_new = _mm(u, w_in, layer, COL_V, D_MODEL, TN_PROJ, tm)
        y_sb = _sb_decode(proj, k_new, v_new, cache_k, cache_v, page_table, lw["sb_bias"], u_ext, layer, bsz, t)

    if state is None:
        ssd_buf = lru_buf = ssd_h0 = lru_h0 = None
    else:
        ssd_h0, ssd_buf, lru_h0, lru_buf = state
    y_ssd_raw, ssd_h = _ssd_scan(proj, dt_raw, bsz, t, rows, lw["ssd_conv_w"], lw["ssd_conv_b"], lw["ssd_dt_bias"],
                                 lw["ssd_a_log"], lw["ssd_d"], ssd_buf, ssd_h0)
    y_ssd = _ssd_gate(y_ssd_raw, proj, lw["ssd_norm_g"], tm)
    y_lru, lru_h = _lru(proj, bsz, t, rows, lw["lru_conv_w"], lw["lru_conv_b"], lw["lru_w_a"], lw["lru_w_x"], layer,
                        lw["lru_b_a"], lw["lru_b_x"], lw["lru_lambda"], lru_buf, lru_h0)
    merged = _merge(y_ssd, y_sb, y_lru, proj, lw["w_branch"], layer, tm, TN_MERGE)
    x = _mm_norm_res(merged, lw["w_out"], layer, x, lw["g_post_mix"], g1, tm, D_MODEL, bpb)
    u2 = _norm_mod(x, lw["g_pre_ffn"], sc2, sh2, tm, bpb)
    hidden = _mm_glu(u2, lw["w_up"], layer, tm, TN_GLU)
    x = _mm_norm_res(hidden, lw["w_down"], layer, x, lw["g_post_ffn"], g2, tm, TK_DOWN, bpb)

    proj3 = proj.reshape(bsz, t, N_MAIN)
    ssd_conv = proj3[:, t - (CONV_W - 1):, COL_XBC:COL_XBC + SSD_XBC]
    lru_conv = proj3[:, t - (CONV_W - 1):, COL_LX:COL_LX + D_MODEL]
    outs = (k_new, v_new, ssd_h.reshape(bsz, SSD_HEADS, SSD_HEAD_DIM, SSD_STATE), ssd_conv,
            lru_h.reshape(bsz, D_MODEL), lru_conv)
    return x, outs


def _row2(v):
    return v.reshape(1, -1)


def kernel(x_prompt, x_sample, c_prompt, c_sample, cache_k, cache_v, page_table, state_ssd, state_ssd_conv, state_lru, state_lru_conv, w_ada, b_ada, g_pre_mix, g_post_mix, g_pre_ffn, g_post_ffn, w_in, ssd_conv_w, ssd_conv_b, ssd_dt_bias, ssd_a_log, ssd_d, ssd_norm_g, sb_bias, lru_conv_w, lru_conv_b, lru_w_a, lru_b_a, lru_w_x, lru_b_x, lru_lambda, w_branch, w_out, w_up, w_down):
    bp, tp, d = x_prompt.shape
    bs, ts, _ = x_sample.shape
    depth = w_in.shape[0]
    n_pool, page = cache_k.shape[1], cache_k.shape[2]
    assert page == SB_BLOCK and tp % 512 == 0 and ts % 8 == 0 and ts < SB_BLOCK

    off_xbc = D_MODEL
    off_dt = off_xbc + SSD_XBC
    off_q = off_dt + SSD_HEADS
    w_in_r = jnp.concatenate(
        [w_in[:, :, 0:D_MODEL], w_in[:, :, off_q:off_q + D_MODEL], w_in[:, :, off_xbc:off_dt],
         w_in[:, :, off_q + 3 * D_MODEL:], w_in[:, :, off_q + D_MODEL:off_q + 3 * D_MODEL],
         w_in[:, :, off_dt:off_q], jnp.zeros((depth, d, LANES - SSD_HEADS), w_in.dtype)], axis=2).astype(BF16)
    w_ada_b = w_ada.astype(BF16)
    w_branch_b = w_branch.astype(BF16)
    w_out_b = w_out.astype(BF16)
    w_up_b = w_up.astype(BF16)
    w_down_b = w_down.astype(BF16)
    lru_w_a_b = lru_w_a.astype(BF16)
    lru_w_x_b = lru_w_x.astype(BF16)
    u_ext = _suffix_matrix()
    cache_k2 = cache_k.reshape(depth * n_pool, page * SB_HEADS, SB_HEAD_DIM)
    cache_v2 = cache_v.reshape(depth * n_pool, page * SB_HEADS, SB_HEAD_DIM)

    xp = x_prompt.reshape(bp * tp, d)
    xs = x_sample.reshape(bs * ts, d)
    n_c = bp + bs
    n_c_pad = -(-n_c // 16) * 16
    c_all = jnp.concatenate([c_prompt, c_sample, jnp.zeros((n_c_pad - n_c, d), c_prompt.dtype)], axis=0)
    tm_p = 512
    cfg_p = (bp, tp, tm_p, tp // tm_p, 512)
    cfg_s = (bs, ts, bs * ts, 1, ts)

    per_layer = []
    kv_slabs = None
    for l in range(depth):
        lw = {
            "g_pre_mix": _row2(g_pre_mix[l]), "g_post_mix": _row2(g_post_mix[l]),
            "g_pre_ffn": _row2(g_pre_ffn[l]), "g_post_ffn": _row2(g_post_ffn[l]),
            "w_in": w_in_r, "ssd_conv_w": ssd_conv_w[l], "ssd_conv_b": _row2(ssd_conv_b[l]),
            "ssd_dt_bias": ssd_dt_bias[l], "ssd_a_log": ssd_a_log[l], "ssd_d": ssd_d[l],
            "ssd_norm_g": _row2(ssd_norm_g[l]), "sb_bias": sb_bias[l],
            "lru_conv_w": lru_conv_w[l], "lru_conv_b": _row2(lru_conv_b[l]),
            "lru_w_a": lru_w_a_b, "lru_w_x": lru_w_x_b, "lru_b_a": _row2(lru_b_a[l]), "lru_b_x": _row2(lru_b_x[l]),
            "lru_lambda": _row2(lru_lambda[l]), "w_branch": w_branch_b, "w_out": w_out_b,
            "w_up": w_up_b, "w_down": w_down_b,
        }
        mod = _mm(c_all, w_ada_b, l, 0, 6 * d, TN_PROJ, n_c_pad, bias=_row2(b_ada[l]), pre_silu=True)
        mods_p = tuple(mod[:bp, i * d:(i + 1) * d].reshape(bp, 1, d) for i in range(6))
        mods_s = tuple(jnp.repeat(mod[bp:n_c, i * d:(i + 1) * d], ts, axis=0).reshape(1, bs * ts, d) for i in range(6))
        xp, outs_p = _layer(xp, mods_p, cfg_p, lw, l, u_ext, kv_slabs=kv_slabs)
        kv_slabs = outs_p[:2]
        state = (state_ssd[l].reshape(bs, SSD_PAIRS, 2 * SSD_HEAD_DIM, SSD_STATE), state_ssd_conv[l],
                 state_lru[l].reshape(bs, 1, d), state_lru_conv[l])
        xs, outs_s = _layer(xs, mods_s, cfg_s, lw, l, u_ext, past=(cache_k2, cache_v2, page_table), state=state)
        per_layer.append((outs_p, outs_s))

    def stack(group, idx):
        return jnp.stack([per_layer[l][group][idx] for l in range(depth)])

    kv_p = (depth, bp, tp, SB_HEADS, SB_HEAD_DIM)
    kv_s = (depth, bs, ts, SB_HEADS, SB_HEAD_DIM)
    return (xp.reshape(bp, tp, d), xs.reshape(bs, ts, d),
            kv_slabs[0].reshape(kv_p), kv_slabs[1].reshape(kv_p), stack(1, 0).reshape(kv_s), stack(1, 1).reshape(kv_s),
            stack(0, 2), stack(1, 2), stack(0, 3), stack(1, 3),
            stack(0, 4), stack(1, 4), stack(0, 5), stack(1, 5))
```

```python
import functools

import jax
import jax.numpy as jnp
from jax import lax
from jax.experimental import pallas as pl
from jax.experimental.pallas import tpu as pltpu

F32 = jnp.float32
BF16 = jnp.bfloat16

D_MODEL = 2048
DEPTH = 4
CONV_W = 4
RMS_EPS = 1e-6
SSD_HEAD_DIM = 64
SSD_HEADS = 32
SSD_GROUPS = 4
SSD_STATE = 128
SSD_CHUNK = 128
SSD_XBC = D_MODEL + 2 * SSD_GROUPS * SSD_STATE
SSD_PAIRS = SSD_HEADS // 2
SB_HEADS = 16
SB_HEAD_DIM = 128
SB_BLOCK = 128
LRU_BLOCKS = 8
LRU_BLOCK_DIM = 256
LRU_C = 8.0
N_BRANCH = 3
D_FF = 5632
LANES = 128
SUBLANES = 8

COL_Z = 0
COL_Q = COL_Z + D_MODEL
COL_XBC = COL_Q + D_MODEL
COL_LX = COL_XBC + SSD_XBC
COL_LG = COL_LX + D_MODEL
COL_GATE = COL_LG + D_MODEL
N_MAIN = COL_GATE + N_BRANCH * D_MODEL
COL_K = N_MAIN
COL_V = COL_K + D_MODEL
COL_DT = COL_V + D_MODEL
N_IN_PAD = COL_DT + LANES

LOG2E = 1.4426950408889634

VMEM_LIMIT = 48 * 1024 * 1024
TN_PROJ = 1024
TN_MERGE = 512
TN_GLU = 512
TK_DOWN = D_FF // 4
SB_TQ = 256
SB_GROUP = 4
SB_SUB = 4
SB_PAGES_PER_STEP = 4


def _cparams(sem):
    return pltpu.CompilerParams(dimension_semantics=sem, vmem_limit_bytes=VMEM_LIMIT)


def _dot(a, b):
    return jnp.dot(a, b, preferred_element_type=F32)


def _dot_nt(a, b):
    return lax.dot_general(a, b, (((1,), (1,)), ((), ())), preferred_element_type=F32)


def _dot_tn(a, b):
    return lax.dot_general(a, b, (((0,), (0,)), ((), ())), preferred_element_type=F32)


def _sigmoid(x):
    return 1.0 / (1.0 + jnp.exp(-x))


def _silu(x):
    return x * _sigmoid(x)


def _softplus(x):
    return jnp.maximum(x, 0.0) + jnp.log1p(jnp.exp(-jnp.abs(x)))


def _gelu_tanh(x):
    return 0.5 * x * (1.0 + jnp.tanh(0.7978845608028654 * (x + 0.044715 * (x * x * x))))


def _split_dot_left(w, x, terms):
    acc = None
    rem = x
    for t in range(terms):
        piece = rem.astype(BF16)
        part = _dot(w, piece)
        acc = part if acc is None else acc + part
        if t + 1 < terms:
            rem = rem - piece.astype(F32)
    return acc


def _mm_kernel(*refs, pre_silu, has_bias, n_in):
    a_ref, w_ref = refs[0], refs[1]
    a = a_ref[...]
    if pre_silu:
        a = _silu(a.astype(F32))
    acc = _dot(a.astype(BF16), w_ref[...].astype(BF16))
    if has_bias:
        acc = acc + refs[2][...]
    for o_ref in refs[n_in:]:
        o_ref[...] = acc.astype(o_ref.dtype)


def _mm(a, w, layer, col0, ncols, tn, tm, bias=None, pre_silu=False, bf16_copy=False, slab=None):
    m, k = a.shape
    assert col0 % tn == 0 and ncols % tn == 0 and m % tm == 0
    cb = col0 // tn
    in_specs = [pl.BlockSpec((tm, k), lambda j, i: (i, 0)),
                pl.BlockSpec((None, k, tn), lambda j, i: (layer, 0, j + cb))]
    args = [a, w]
    if bias is not None:
        in_specs.append(pl.BlockSpec((1, tn), lambda j, i: (0, j + cb)))
        args.append(bias)
    aliases = {}
    rb, out_rows = 0, m
    if slab is not None:
        dest, n_slabs, index = slab
        rb, out_rows = index * (m // tm), n_slabs * m
        if dest is not None:
            aliases = {len(args): 0}
            in_specs.append(pl.BlockSpec(memory_space=pl.ANY))
            args.append(dest)
    out_specs = [pl.BlockSpec((tm, tn), lambda j, i: (i + rb, j))]
    out_shape = [jax.ShapeDtypeStruct((out_rows, ncols), F32)]
    if bf16_copy:
        out_specs.append(pl.BlockSpec((tm, tn), lambda j, i: (i, j)))
        out_shape.append(jax.ShapeDtypeStruct((m, ncols), BF16))
    outs = pl.pallas_call(
        functools.partial(_mm_kernel, pre_silu=pre_silu, has_bias=bias is not None, n_in=len(args)),
        grid=(ncols // tn, m // tm),
        in_specs=in_specs,
        out_specs=out_specs,
        out_shape=out_shape,
        input_output_aliases=aliases,
        compiler_params=_cparams(("parallel", "parallel")),
        name="mm",
    )(*args)
    return outs if bf16_copy else outs[0]


REF_XBC = D_MODEL
REF_DT = REF_XBC + SSD_XBC
REF_Q = REF_DT + SSD_HEADS
REF_K = REF_Q + D_MODEL
REF_V = REF_K + D_MODEL
REF_LX = REF_V + D_MODEL
N_IN = REF_LX + 2 * D_MODEL + N_BRANCH * D_MODEL
DT_SHIFT = REF_Q % LANES


_REPACK_SEGMENTS = ((COL_Z, 0), (COL_Q, REF_Q), (COL_XBC, REF_XBC), (COL_LX, REF_LX), (COL_K, REF_K), (COL_V, REF_V),
                    (COL_DT, REF_DT))


def _repack_src_block(j):
    src = 0
    for start, ref_start in _REPACK_SEGMENTS:
        src = jnp.where(j >= start // LANES, ref_start // LANES + (j - start // LANES), src)
    return src


def _repack_kernel(a_ref, b_ref, o_ref):
    j = pl.program_id(1)
    aligned = (j < COL_Q // LANES) | ((j >= COL_XBC // LANES) & (j < COL_LX // LANES))
    is_dt = j == COL_DT // LANES

    @pl.when(aligned)
    def _():
        o_ref[...] = a_ref[...].astype(BF16)

    @pl.when(is_dt)
    def _():
        pad = jnp.zeros((a_ref.shape[0], LANES - SSD_HEADS), F32)
        o_ref[...] = jnp.concatenate([a_ref[:, :SSD_HEADS], pad], axis=1).astype(BF16)

    @pl.when(jnp.logical_not(aligned | is_dt))
    def _():
        o_ref[...] = jnp.concatenate([a_ref[:, DT_SHIFT:], b_ref[:, :DT_SHIFT]], axis=1).astype(BF16)


def _repack_w_in(w_in):
    depth, k, n = w_in.shape
    assert n == N_IN and DT_SHIFT == SSD_HEADS
    last = (n - 1) // LANES
    return pl.pallas_call(
        _repack_kernel,
        grid=(depth, N_IN_PAD // LANES),
        in_specs=[pl.BlockSpec((None, k, LANES), lambda l, j: (l, 0, _repack_src_block(j))),
                  pl.BlockSpec((None, k, LANES), lambda l, j: (l, 0, jnp.minimum(_repack_src_block(j) + 1, last)))],
        out_specs=pl.BlockSpec((None, k, LANES), lambda l, j: (l, 0, j)),
        out_shape=jax.ShapeDtypeStruct((depth, k, N_IN_PAD), BF16),
        compiler_params=_cparams(("parallel", "parallel")),
        name="repack_w_in",
    )(w_in, w_in)


def _mod_spec(mod, bpb):
    _, r, d = mod.shape
    return pl.BlockSpec((1, r, d), lambda i, *_: (i // bpb, 0, 0))


def _norm_mod_kernel(x_ref, g_ref, sc_ref, sh_ref, o_ref):
    x = x_ref[...]
    ms = jnp.mean(x * x, axis=-1, keepdims=True)
    y = x * lax.rsqrt(ms + RMS_EPS) * g_ref[...]
    o_ref[...] = (y * (1.0 + sc_ref[0]) + sh_ref[0]).astype(o_ref.dtype)


def _norm_mod(x, g, scale, shift, tm, bpb):
    m, d = x.shape
    return pl.pallas_call(
        _norm_mod_kernel,
        grid=(m // tm,),
        in_specs=[pl.BlockSpec((tm, d), lambda i: (i, 0)),
                  pl.BlockSpec((1, d), lambda i: (0, 0)),
                  _mod_spec(scale, bpb), _mod_spec(shift, bpb)],
        out_specs=pl.BlockSpec((tm, d), lambda i: (i, 0)),
        out_shape=jax.ShapeDtypeStruct((m, d), BF16),
        compiler_params=_cparams(("parallel",)),
        name="norm_mod",
    )(x, g, scale, shift)


def _mm_norm_res_kernel(a_ref, w_ref, x_ref, g_ref, gm_ref, o_ref, acc_ref, *, nk):
    k = pl.program_id(1)

    @pl.when(k == 0)
    def _():
        acc_ref[...] = jnp.zeros_like(acc_ref)

    acc_ref[...] += _dot(a_ref[...], w_ref[...])

    @pl.when(k == nk - 1)
    def _():
        mix = acc_ref[...]
        ms = jnp.mean(mix * mix, axis=-1, keepdims=True)
        o_ref[...] = x_ref[...] + gm_ref[0] * (mix * lax.rsqrt(ms + RMS_EPS) * g_ref[...])


def _mm_norm_res(a, w, layer, x, g, gmod, tm, tk, bpb):
    m, kdim = a.shape
    d = w.shape[2]
    nk = kdim // tk
    return pl.pallas_call(
        functools.partial(_mm_norm_res_kernel, nk=nk),
        grid=(m // tm, nk),
        in_specs=[pl.BlockSpec((tm, tk), lambda i, k: (i, k)),
                  pl.BlockSpec((None, tk, d), lambda i, k: (layer, k, 0)),
                  pl.BlockSpec((tm, d), lambda i, k: (i, 0)),
                  pl.BlockSpec((1, d), lambda i, k: (0, 0)),
                  _mod_spec(gmod, bpb)],
        out_specs=pl.BlockSpec((tm, d), lambda i, k: (i, 0)),
        out_shape=jax.ShapeDtypeStruct((m, d), F32),
        scratch_shapes=[pltpu.VMEM((tm, d), F32)],
        compiler_params=_cparams(("parallel", "arbitrary")),
        name="mm_norm_res",
    )(a, w, x, g, gmod)


def _mm_glu_kernel(a_ref, wg_ref, wu_ref, o_ref, wb_ref):
    @pl.when(pl.program_id(1) == 0)
    def _():
        wb_ref[0] = wg_ref[...].astype(BF16)
        wb_ref[1] = wu_ref[...].astype(BF16)

    a = a_ref[...]
    gate = _dot(a, wb_ref[0])
    up = _dot(a, wb_ref[1])
    o_ref[...] = (_silu(gate) * up).astype(o_ref.dtype)


def _mm_glu(a, w_up, layer, tm, tn):
    m, k = a.shape
    nb = D_FF // tn
    return pl.pallas_call(
        _mm_glu_kernel,
        grid=(nb, m // tm),
        in_specs=[pl.BlockSpec((tm, k), lambda j, i: (i, 0)),
                  pl.BlockSpec((None, k, tn), lambda j, i: (layer, 0, j)),
                  pl.BlockSpec((None, k, tn), lambda j, i: (layer, 0, j + nb))],
        out_specs=pl.BlockSpec((tm, tn), lambda j, i: (i, j)),
        out_shape=jax.ShapeDtypeStruct((m, D_FF), BF16),
        scratch_shapes=[pltpu.VMEM((2, k, tn), BF16)],
        compiler_params=_cparams(("parallel", "arbitrary")),
        name="mm_glu",
    )(a, w_up, w_up)


def _merge_kernel(y0_ref, y1_ref, y2_ref, w_ref, g0_ref, g1_ref, g2_ref, o_ref):
    acc = _sigmoid(g0_ref[...]) * _dot(y0_ref[...].astype(BF16), w_ref[0])
    acc = acc + _sigmoid(g1_ref[...]) * _dot(y1_ref[...].astype(BF16), w_ref[1])
    acc = acc + _sigmoid(g2_ref[...]) * _dot(y2_ref[...].astype(BF16), w_ref[2])
    o_ref[...] = acc.astype(o_ref.dtype)


def _merge(y_ssd, y_sb, y_lru, proj, w_branch, layer, tm, tn):
    m, d = y_ssd.shape
    gb = COL_GATE // tn
    nb = d // tn
    y_spec = pl.BlockSpec((tm, d), lambda i, j: (i, 0))
    return pl.pallas_call(
        _merge_kernel,
        grid=(m // tm, nb),
        in_specs=[y_spec, y_spec, y_spec,
                  pl.BlockSpec((None, N_BRANCH, d, tn), lambda i, j: (layer, 0, 0, j)),
                  pl.BlockSpec((tm, tn), lambda i, j: (i, gb + j)),
                  pl.BlockSpec((tm, tn), lambda i, j: (i, gb + nb + j)),
                  pl.BlockSpec((tm, tn), lambda i, j: (i, gb + 2 * nb + j))],
        out_specs=pl.BlockSpec((tm, tn), lambda i, j: (i, j)),
        out_shape=jax.ShapeDtypeStruct((m, d), BF16),
        compiler_params=_cparams(("parallel", "parallel")),
        name="merge",
    )(y_ssd, y_sb, y_lru, w_branch, proj, proj, proj)


def _conv_from_scratch(xp_ref, s, raw, w, b, rows):
    xp_ref[s, 8:8 + rows, :] = raw
    y = b + w[3:4] * raw
    for j in range(CONV_W - 1):
        y = y + w[j:j + 1] * xp_ref[s, 5 + j:5 + j + rows, :]
    if rows >= 8:
        xp_ref[s, 0:8, :] = raw[rows - 8:rows]
    return y


def _ssd_kernel(*refs, rows, has_init):
    if has_init:
        (dtb_ref, alog_ref, dsk_ref, x_ref, bm_ref, cm_ref, dt_ref, cwx_ref, cwb_ref, cwc_ref, cbx_ref, cbb_ref, cbc_ref,
         bufx_ref, bufb_ref, bufc_ref, h0_ref, y_ref, hout_ref, h_ref, xp_ref) = refs
    else:
        (dtb_ref, alog_ref, dsk_ref, x_ref, bm_ref, cm_ref, dt_ref, cwx_ref, cwb_ref, cwc_ref, cbx_ref, cbb_ref, cbc_ref,
         y_ref, hout_ref, h_ref, xp_ref) = refs
    p = pl.program_id(1)
    r = pl.program_id(2)
    nr = pl.num_programs(2)
    ln = SSD_CHUNK

    @pl.when(r == 0)
    def _():
        if has_init:
            h_ref[...] = h0_ref[0, 0]
            xp_ref[0, 5:8, :] = bufx_ref[0]
            xp_ref[1, 5:8, :] = bufb_ref[0]
            xp_ref[2, 5:8, :] = bufc_ref[0]
        else:
            h_ref[...] = jnp.zeros_like(h_ref)
            xp_ref[:, 0:8, :] = jnp.zeros((3, 8, LANES), F32)

    xs_all = _silu(_conv_from_scratch(xp_ref, 0, x_ref[...], cwx_ref[...], cbx_ref[...], rows))
    bm_all = _silu(_conv_from_scratch(xp_ref, 1, bm_ref[...], cwb_ref[...], cbb_ref[...], rows))
    cm_all = _silu(_conv_from_scratch(xp_ref, 2, cm_ref[...], cwc_ref[...], cbc_ref[...], rows))
    dt_all = dt_ref[...]

    lane = lax.broadcasted_iota(jnp.int32, (ln, LANES), 1)
    row = lax.broadcasted_iota(jnp.int32, (ln, LANES), 0)
    first_half = lane < SSD_HEAD_DIM
    tril = row >= lane
    eye = row == lane
    tri_bf = jnp.where(tril, 1.0, 0.0).astype(BF16)

    h0i = 2 * p
    h1i = 2 * p + 1
    a0 = -jnp.exp(jnp.zeros((1, LANES), F32) + alog_ref[h0i])
    a1 = -jnp.exp(jnp.zeros((1, LANES), F32) + alog_ref[h1i])
    dskip = jnp.where(first_half[0:1], dsk_ref[h0i], dsk_ref[h1i])

    n_chunks = max(rows // ln, 1)
    out_rows = min(rows, ln)

    chunks = []
    for c in range(n_chunks):
        if rows >= ln:
            xs = xs_all[c * ln:(c + 1) * ln]
            bm = bm_all[c * ln:(c + 1) * ln]
            cm = cm_all[c * ln:(c + 1) * ln]
            dtr = dt_all[c * ln:(c + 1) * ln]
            valid = None
        else:
            pad = jnp.zeros((ln - rows, LANES), F32)
            xs = jnp.concatenate([xs_all, pad], axis=0)
            bm = jnp.concatenate([bm_all, pad], axis=0)
            cm = jnp.concatenate([cm_all, pad], axis=0)
            dtr = jnp.concatenate([dt_all, pad], axis=0)
            valid = row < rows

        dtr0 = jnp.sum(jnp.where(lane == h0i, dtr, 0.0), axis=1, keepdims=True)
        dtr1 = jnp.sum(jnp.where(lane == h1i, dtr, 0.0), axis=1, keepdims=True)
        dt0 = _softplus(jnp.broadcast_to(dtr0, (ln, LANES)) + dtb_ref[h0i])
        dt1 = _softplus(jnp.broadcast_to(dtr1, (ln, LANES)) + dtb_ref[h1i])
        if valid is not None:
            dt0 = jnp.where(valid, dt0, 0.0)
            dt1 = jnp.where(valid, dt1, 0.0)
        la = jnp.concatenate([dt0 * a0, dt1 * a1], axis=1)
        acum = _split_dot_left(tri_bf, la, 3)
        cm_bf = cm.astype(BF16)
        bm_bf = bm.astype(BF16)
        cb = _dot_nt(cm_bf, bm_bf)
        chunks.append((xs, bm_bf, cm_bf, dt0, dt1, acum, cb))

    staged = []
    for xs, bm_bf, cm_bf, dt0, dt1, acum, cb in chunks:
        ac0 = acum[:, :LANES]
        ac1 = acum[:, LANES:]
        acp = jnp.where(first_half, ac0, ac1)
        dtp = jnp.where(first_half, dt0, dt1)
        last0 = ac0[ln - 1:ln]
        last1 = ac1[ln - 1:ln]
        lastp = acp[ln - 1:ln]
        xs_bf = xs.astype(BF16)

        def scores(ac, dt, cb=cb):
            arow = jnp.sum(jnp.where(eye, ac, 0.0), axis=0, keepdims=True)
            dtrow = jnp.sum(jnp.where(eye, dt, 0.0), axis=0, keepdims=True)
            decay = jnp.exp(jnp.where(tril, ac - arow, -jnp.inf))
            return (cb * decay * dtrow).astype(BF16)

        y_diag = jnp.where(first_half, _dot(scores(ac0, dt0), xs_bf), _dot(scores(ac1, dt1), xs_bf))
        w_state = jnp.exp(lastp - acp) * dtp
        states = _dot_tn((xs * w_state).astype(BF16), bm_bf)
        cdec = jnp.where(row < SSD_HEAD_DIM, jnp.broadcast_to(last0, (ln, LANES)), jnp.broadcast_to(last1, (ln, LANES)))
        staged.append((cm_bf, jnp.exp(acp), y_diag + dskip * xs, jnp.exp(cdec), states))

    h = h_ref[...]
    for c, (cm_bf, eacp, y_local, edec, states) in enumerate(staged):
        y = y_local + eacp * _dot_nt(cm_bf, h.astype(BF16))
        y_ref[c * ln:c * ln + out_rows, :] = y[:out_rows]
        h = edec * h + states
    h_ref[...] = h

    @pl.when(r == nr - 1)
    def _():
        hout_ref[0, 0] = h_ref[...]


def _ssd_scan(proj, dt_raw, bsz, t, rows, conv_w, conv_b, dt_bias, a_log, d_skip, conv_buf=None, h0=None):
    m = bsz * t
    nrb = t // rows
    has_init = h0 is not None
    cx = COL_XBC // LANES
    cbm = cx + D_MODEL // LANES
    ccm = cbm + SSD_GROUPS
    hpg = SSD_PAIRS // SSD_GROUPS

    def rowmap(off_fn):
        return lambda b, p, r, *_: (b * nrb + r, off_fn(p))

    def wmap(off_fn):
        return lambda b, p, r, *_: (0, off_fn(p))

    def bufmap(off_fn):
        return lambda b, p, r, *_: (b, 0, off_fn(p))

    fx = lambda p: p
    fb = lambda p: D_MODEL // LANES + p // hpg
    fc = lambda p: D_MODEL // LANES + SSD_GROUPS + p // hpg
    in_specs = [
        pl.BlockSpec((rows, LANES), rowmap(lambda p: cx + p)),
        pl.BlockSpec((rows, LANES), rowmap(lambda p: cbm + p // hpg)),
        pl.BlockSpec((rows, LANES), rowmap(lambda p: ccm + p // hpg)),
        pl.BlockSpec((rows, LANES), rowmap(lambda p: 0)),
        pl.BlockSpec((CONV_W, LANES), wmap(fx)), pl.BlockSpec((CONV_W, LANES), wmap(fb)), pl.BlockSpec((CONV_W, LANES), wmap(fc)),
        pl.BlockSpec((1, LANES), wmap(fx)), pl.BlockSpec((1, LANES), wmap(fb)), pl.BlockSpec((1, LANES), wmap(fc)),
    ]
    args = [proj, proj, proj, dt_raw, conv_w, conv_w, conv_w, conv_b, conv_b, conv_b]
    if has_init:
        in_specs += [pl.BlockSpec((1, CONV_W - 1, LANES), bufmap(fx)),
                     pl.BlockSpec((1, CONV_W - 1, LANES), bufmap(fb)),
                     pl.BlockSpec((1, CONV_W - 1, LANES), bufmap(fc)),
                     pl.BlockSpec((1, 1, LANES, SSD_STATE), lambda b, p, r, *_: (b, p, 0, 0))]
        args += [conv_buf, conv_buf, conv_buf, h0]
    grid_spec = pltpu.PrefetchScalarGridSpec(
        num_scalar_prefetch=3,
        grid=(bsz, SSD_PAIRS, nrb),
        in_specs=in_specs,
        out_specs=[pl.BlockSpec((rows, LANES), rowmap(lambda p: p)),
                   pl.BlockSpec((1, 1, LANES, SSD_STATE), lambda b, p, r, *_: (b, p, 0, 0))],
        scratch_shapes=[pltpu.VMEM((LANES, SSD_STATE), F32), pltpu.VMEM((3, rows + 8, LANES), F32)],
    )
    return pl.pallas_call(
        functools.partial(_ssd_kernel, rows=rows, has_init=has_init),
        grid_spec=grid_spec,
        out_shape=[jax.ShapeDtypeStruct((m, D_MODEL), F32),
                   jax.ShapeDtypeStruct((bsz, SSD_PAIRS, LANES, SSD_STATE), F32)],
        compiler_params=_cparams(("parallel", "parallel", "arbitrary")),
        name="ssd_scan",
    )(dt_bias, a_log, d_skip, *args)


def _ssd_gate_kernel(y_ref, z_ref, g_ref, o_ref):
    y = y_ref[...] * _silu(z_ref[...])
    ms = jnp.mean(y * y, axis=-1, keepdims=True)
    o_ref[...] = (y * lax.rsqrt(ms + RMS_EPS) * g_ref[...]).astype(o_ref.dtype)


def _ssd_gate(y, proj, norm_g, tm):
    m, d = y.shape
    return pl.pallas_call(
        _ssd_gate_kernel,
        grid=(m // tm,),
        in_specs=[pl.BlockSpec((tm, d), lambda i: (i, 0)),
                  pl.BlockSpec((tm, d), lambda i: (i, COL_Z // D_MODEL)),
                  pl.BlockSpec((1, d), lambda i: (0, 0))],
        out_specs=pl.BlockSpec((tm, d), lambda i: (i, 0)),
        out_shape=jax.ShapeDtypeStruct((m, d), BF16),
        compiler_params=_cparams(("parallel",)),
        name="ssd_gate",
    )(y, proj, norm_g)


def _lru_kernel(*refs, rows, has_init):
    if has_init:
        (x_ref, g_ref, cw_ref, cb_ref, wa_ref, wx_ref, ba_ref, bx_ref, lam_ref, buf_ref, h0_ref,
         y_ref, hout_ref, h_ref, xp_ref) = refs
    else:
        (x_ref, g_ref, cw_ref, cb_ref, wa_ref, wx_ref, ba_ref, bx_ref, lam_ref,
         y_ref, hout_ref, h_ref, xp_ref) = refs
    r = pl.program_id(2)
    nr = pl.num_programs(2)
    w = LRU_BLOCK_DIM

    @pl.when(r == 0)
    def _():
        if has_init:
            h_ref[...] = h0_ref[0]
            xp_ref[0, 5:8, :] = buf_ref[0]
        else:
            h_ref[...] = jnp.zeros_like(h_ref)
            xp_ref[0, 0:8, :] = jnp.zeros((8, w), F32)

    xc = _conv_from_scratch(xp_ref, 0, x_ref[...], cw_ref[...], cb_ref[...], rows)
    xb = xc.astype(BF16)
    rg = _sigmoid(_dot(xb, wa_ref[0]) + ba_ref[...])
    ig = _sigmoid(_dot(xb, wx_ref[0]) + bx_ref[...])
    log_a = -LRU_C * rg * _softplus(-lam_ref[...])
    a = jnp.exp(log_a)
    one_m_a2 = -jnp.tanh(log_a) * (a * a + 1.0)
    b = jnp.where(one_m_a2 > 0.0, one_m_a2 * lax.rsqrt(one_m_a2), 0.0) * (ig * xc)

    row = lax.broadcasted_iota(jnp.int32, (rows, w), 0) % SUBLANES
    d = 1
    while d < SUBLANES:
        keep = row >= d
        a_sh = jnp.where(keep, pltpu.roll(a, d, 0), 1.0)
        b_sh = jnp.where(keep, pltpu.roll(b, d, 0), 0.0)
        b = a * b_sh + b
        a = a * a_sh
        d *= 2
    h_in = h_ref[...]
    entering = []
    for g in range(rows // SUBLANES):
        entering.append(jnp.broadcast_to(h_in, (SUBLANES, w)))
        last = g * SUBLANES + SUBLANES - 1
        h_in = a[last:last + 1] * h_in + b[last:last + 1]
    h = a * jnp.concatenate(entering, axis=0) + b
    h_ref[...] = h_in
    y_ref[...] = (_gelu_tanh(g_ref[...]) * h).astype(y_ref.dtype)

    @pl.when(r == nr - 1)
    def _():
        hout_ref[0] = h[rows - 1:rows]


def _lru(proj, bsz, t, rows, conv_w, conv_b, w_a, w_x, layer, b_a, b_x, lam, conv_buf=None, h0=None):
    m = bsz * t
    nrb = t // rows
    w = LRU_BLOCK_DIM
    has_init = h0 is not None
    cx = COL_LX // w
    cg = COL_LG // w
    vec = lambda b, k, r: (0, k)
    in_specs = [pl.BlockSpec((rows, w), lambda b, k, r: (b * nrb + r, cx + k)),
                pl.BlockSpec((rows, w), lambda b, k, r: (b * nrb + r, cg + k)),
                pl.BlockSpec((CONV_W, w), vec), pl.BlockSpec((1, w), vec),
                pl.BlockSpec((None, 1, w, w), lambda b, k, r: (layer, k, 0, 0)),
                pl.BlockSpec((None, 1, w, w), lambda b, k, r: (layer, k, 0, 0)),
                pl.BlockSpec((1, w), vec), pl.BlockSpec((1, w), vec), pl.BlockSpec((1, w), vec)]
    args = [proj, proj, conv_w, conv_b, w_a, w_x, b_a, b_x, lam]
    if has_init:
        in_specs += [pl.BlockSpec((1, CONV_W - 1, w), lambda b, k, r: (b, 0, k)),
                     pl.BlockSpec((1, 1, w), lambda b, k, r: (b, 0, k))]
        args += [conv_buf, h0]
    return pl.pallas_call(
        functools.partial(_lru_kernel, rows=rows, has_init=has_init),
        grid=(bsz, LRU_BLOCKS, nrb),
        in_specs=in_specs,
        out_specs=[pl.BlockSpec((rows, w), lambda b, k, r: (b * nrb + r, k)),
                   pl.BlockSpec((1, 1, w), lambda b, k, r: (b, 0, k))],
        out_shape=[jax.ShapeDtypeStruct((m, D_MODEL), BF16 if rows % 16 == 0 else F32),
                   jax.ShapeDtypeStruct((bsz, 1, D_MODEL), F32)],
        scratch_shapes=[pltpu.VMEM((1, w), F32), pltpu.VMEM((1, rows + 8, w), F32)],
        compiler_params=_cparams(("parallel", "parallel", "arbitrary")),
        name="lru",
    )(*args)


def _suffix_matrix():
    r = lax.broadcasted_iota(jnp.int32, (2 * SB_BLOCK, 2 * SB_BLOCK), 0) % SB_BLOCK
    c = lax.broadcasted_iota(jnp.int32, (2 * SB_BLOCK, 2 * SB_BLOCK), 1)
    return jnp.where((r >= c) | (c >= SB_BLOCK), 1.0, 0.0).astype(BF16)


def _sb_scores(s, scale2, bias2, mask):
    z2 = s * scale2 + bias2
    sp = jnp.maximum(z2, 0.0) + jnp.log2(1.0 + jnp.exp2(-jnp.abs(z2)))
    if mask is not None:
        sp = jnp.where(mask, sp, 0.0)
    return z2, sp


def _sb_suffix(sp, u_ext):
    hi = sp.astype(BF16)
    lo = (sp - hi.astype(F32)).astype(BF16)
    return _dot(jnp.concatenate([hi, lo], axis=1), u_ext)


def _sb_prompt_kernel(bias_ref, q_ref, k_ref, v_ref, u_ref, o_ref, acc_ref, carry_ref, *, g_heads, n_sub):
    hg = pl.program_id(1)
    i = pl.program_id(2)
    tq = SB_TQ
    wide = n_sub * SB_BLOCK
    u_ext = u_ref[...]
    scale2 = (SB_HEAD_DIM ** -0.5) * LOG2E
    qbs = [q_ref[:, g * SB_HEAD_DIM:(g + 1) * SB_HEAD_DIM].astype(BF16) for g in range(g_heads)]
    biases = [bias_ref[hg * g_heads + g] * LOG2E for g in range(g_heads)]

    def stripe(st, n_act, masked):
        span = n_act * SB_BLOCK
        rows = pl.ds(pl.multiple_of(st * wide, wide), span)
        if masked:
            row = lax.broadcasted_iota(jnp.int32, (tq, span), 0)
            col = lax.broadcasted_iota(jnp.int32, (tq, span), 1)
            mask = (st * wide + col) < (i * tq + row)
        else:
            mask = None
        zs = [_dot_nt(qbs[g], k_ref[rows, g * SB_HEAD_DIM:(g + 1) * SB_HEAD_DIM]) for g in range(g_heads)]
        z2s, sufs = [], []
        for g in range(g_heads):
            z2, sp = _sb_scores(zs[g], scale2, biases[g], mask)
            z2s.append(z2)
            sufs.append([_sb_suffix(sp[:, s * SB_BLOCK:(s + 1) * SB_BLOCK], u_ext) for s in range(n_act)])
        for g in range(g_heads):
            carry = carry_ref[g]
            parts = [None] * n_act
            for s in reversed(range(n_act)):
                suf = sufs[g][s]
                parts[s] = jnp.exp2(z2s[g][:, s * SB_BLOCK:(s + 1) * SB_BLOCK] - suf[:, :SB_BLOCK] - carry)
                carry = carry + suf[:, SB_BLOCK:]
            carry_ref[g] = carry
            w = jnp.concatenate(parts, axis=1)
            if masked:
                w = jnp.where(mask, w, 0.0)
            acc_ref[g] += _dot(w.astype(BF16), v_ref[rows, g * SB_HEAD_DIM:(g + 1) * SB_HEAD_DIM])

    acc_ref[...] = jnp.zeros_like(acc_ref)
    carry_ref[...] = jnp.zeros_like(carry_ref)
    diag = (i * tq) // wide
    q_per_stripe = wide // tq
    for r in range(q_per_stripe):
        @pl.when(i % q_per_stripe == r)
        def _(r=r):
            stripe(diag, (r + 1) * tq // SB_BLOCK, True)

    def body(step, c):
        stripe(diag - 1 - step, n_sub, False)
        return c

    lax.fori_loop(0, diag, body, 0)
    for g in range(g_heads):
        o_ref[:, g * SB_HEAD_DIM:(g + 1) * SB_HEAD_DIM] = acc_ref[g].astype(o_ref.dtype)


def _sb_prompt(proj, k_bf, v_bf, sb_bias, u_ext, bsz, t):
    m = bsz * t
    nq = t // SB_TQ
    w = SB_GROUP * SB_HEAD_DIM
    cq = COL_Q // w
    assert t % (SB_SUB * SB_BLOCK) == 0 and (SB_SUB * SB_BLOCK) % SB_TQ == 0
    grid_spec = pltpu.PrefetchScalarGridSpec(
        num_scalar_prefetch=1,
        grid=(bsz, SB_HEADS // SB_GROUP, nq),
        in_specs=[pl.BlockSpec((SB_TQ, w), lambda b, h, i, *_: (b * nq + i, cq + h)),
                  pl.BlockSpec((t, w), lambda b, h, i, *_: (b, h)),
                  pl.BlockSpec((t, w), lambda b, h, i, *_: (b, h)),
                  pl.BlockSpec((2 * SB_BLOCK, 2 * SB_BLOCK), lambda b, h, i, *_: (0, 0))],
        out_specs=pl.BlockSpec((SB_TQ, w), lambda b, h, i, *_: (b * nq + i, h)),
        scratch_shapes=[pltpu.VMEM((SB_GROUP, SB_TQ, SB_HEAD_DIM), F32),
                        pltpu.VMEM((SB_GROUP, SB_TQ, SB_BLOCK), F32)],
    )
    return pl.pallas_call(
        functools.partial(_sb_prompt_kernel, g_heads=SB_GROUP, n_sub=SB_SUB),
        grid_spec=grid_spec,
        out_shape=jax.ShapeDtypeStruct((m, D_MODEL), BF16),
        compiler_params=_cparams(("parallel", "parallel", "arbitrary")),
        name="sb_prompt",
    )(sb_bias, proj, k_bf, v_bf, u_ext)


def _sb_decode_kernel(pt_ref, bias_ref, q_ref, kn_ref, vn_ref, *rest, tq, n_pg):
    kc_refs = rest[:n_pg]
    vc_refs = rest[n_pg:2 * n_pg]
    u_ref, o_ref, qf_ref, acc_ref, carry_ref = rest[2 * n_pg:]
    j = pl.program_id(1)
    nj = pl.num_programs(1)
    nh = SB_HEADS
    u_ext = u_ref[...]
    rows = nh * tq
    scale2 = (SB_HEAD_DIM ** -0.5) * LOG2E
    bias2 = jnp.concatenate([jnp.zeros((tq, SB_BLOCK), F32) + bias_ref[h] * LOG2E for h in range(nh)], axis=0)

    def process(k_getters, v_getters, mask):
        qf = qf_ref[...]
        qhs = [qf[h * tq:(h + 1) * tq].astype(BF16) for h in range(nh)]
        zs = [jnp.concatenate([_dot_nt(qhs[h], get_k(h)) for h in range(nh)], axis=0) for get_k in k_getters]
        z2s, sufs = [], []
        for z in zs:
            z2, sp = _sb_scores(z, scale2, bias2, mask)
            z2s.append(z2)
            sufs.append(_sb_suffix(sp, u_ext))
        carry = carry_ref[...]
        acc = acc_ref[...]
        for z2, suf, get_v in zip(z2s, sufs, v_getters):
            w = jnp.exp2(z2 - suf[:, :SB_BLOCK] - carry)
            carry = carry + suf[:, SB_BLOCK:]
            if mask is not None:
                w = jnp.where(mask, w, 0.0)
            acc = acc + jnp.concatenate([_dot(w[h * tq:(h + 1) * tq].astype(BF16), get_v(h)) for h in range(nh)], axis=0)
        carry_ref[...] = carry
        acc_ref[...] = acc

    def page_tile(ref, h):
        return ref[0, pl.ds(h, SB_BLOCK, stride=nh), :].astype(BF16)

    @pl.when(j == 0)
    def _():
        q = q_ref[...]
        qf_ref[...] = jnp.concatenate([q[:, h * SB_HEAD_DIM:(h + 1) * SB_HEAD_DIM] for h in range(nh)], axis=0)
        acc_ref[...] = jnp.zeros_like(acc_ref)
        carry_ref[...] = jnp.zeros_like(carry_ref)
        pad = jnp.zeros((SB_BLOCK - tq, SB_HEAD_DIM), F32)
        kn = kn_ref[...]
        vn = vn_ref[...]
        qi = lax.broadcasted_iota(jnp.int32, (rows, SB_BLOCK), 0) % tq
        ki = lax.broadcasted_iota(jnp.int32, (rows, SB_BLOCK), 1)
        process([lambda h: jnp.concatenate([kn[:, h * SB_HEAD_DIM:(h + 1) * SB_HEAD_DIM], pad], axis=0).astype(BF16)],
                [lambda h: jnp.concatenate([vn[:, h * SB_HEAD_DIM:(h + 1) * SB_HEAD_DIM], pad], axis=0).astype(BF16)],
                ki < qi)

    @pl.when(j > 0)
    def _():
        process([functools.partial(page_tile, r) for r in kc_refs],
                [functools.partial(page_tile, r) for r in vc_refs],
                None)

    @pl.when(j == nj - 1)
    def _():
        acc = acc_ref[...]
        for h in range(nh):
            o_ref[:, h * SB_HEAD_DIM:(h + 1) * SB_HEAD_DIM] = acc[h * tq:(h + 1) * tq].astype(o_ref.dtype)


def _sb_decode(proj, k_new, v_new, cache_k, cache_v, page_table, sb_bias, u_ext, layer, bsz, tq):
    n_pages = page_table.shape[1]
    n_pool = cache_k.shape[0] // DEPTH
    page_rows = cache_k.shape[1]
    base = layer * n_pool
    n_pg = SB_PAGES_PER_STEP
    assert n_pages % n_pg == 0

    def page_map(r):
        return lambda b, j, pt, bias: (base + pt[b, n_pages - n_pg * jnp.maximum(j, 1) + (n_pg - 1 - r)], 0, 0)

    page_specs = [pl.BlockSpec((1, page_rows, SB_HEAD_DIM), page_map(r)) for r in range(n_pg)]
    grid_spec = pltpu.PrefetchScalarGridSpec(
        num_scalar_prefetch=2,
        grid=(bsz, n_pages // n_pg + 1),
        in_specs=[pl.BlockSpec((tq, D_MODEL), lambda b, j, *_: (b, COL_Q // D_MODEL)),
                  pl.BlockSpec((tq, D_MODEL), lambda b, j, *_: (b, 0)),
                  pl.BlockSpec((tq, D_MODEL), lambda b, j, *_: (b, 0))]
                 + page_specs + page_specs
                 + [pl.BlockSpec((2 * SB_BLOCK, 2 * SB_BLOCK), lambda b, j, *_: (0, 0))],
        out_specs=pl.BlockSpec((tq, D_MODEL), lambda b, j, *_: (b, 0)),
        scratch_shapes=[pltpu.VMEM((SB_HEADS * tq, SB_HEAD_DIM), F32),
                        pltpu.VMEM((SB_HEADS * tq, SB_HEAD_DIM), F32),
                        pltpu.VMEM((SB_HEADS * tq, SB_BLOCK), F32)],
    )
    return pl.pallas_call(
        functools.partial(_sb_decode_kernel, tq=tq, n_pg=n_pg),
        grid_spec=grid_spec,
        out_shape=jax.ShapeDtypeStruct((bsz * tq, D_MODEL), F32),
        compiler_params=_cparams(("parallel", "arbitrary")),
        name="sb_decode",
    )(page_table, sb_bias, proj, k_new, v_new, *([cache_k] * n_pg), *([cache_v] * n_pg), u_ext)


def _layer(x, mods, cfg, lw, layer, u_ext, kv_slabs=None, past=None, state=None):
    bsz, t, tm, bpb, rows = cfg
    sh1, sc1, g1, sh2, sc2, g2 = mods
    w_in = lw["w_in"]
    u = _norm_mod(x, lw["g_pre_mix"], sc1, sh1, tm, bpb)
    proj = _mm(u, w_in, layer, 0, N_MAIN, TN_PROJ, tm)
    dt_raw = _mm(u, w_in, layer, COL_DT, LANES, LANES, tm)
    if past is None:
        k_dest, v_dest = kv_slabs if kv_slabs is not None else (None, None)
        k_new, k_bf = _mm(u, w_in, layer, COL_K, D_MODEL, TN_PROJ, tm, bf16_copy=True, slab=(k_dest, DEPTH, layer))
        v_new, v_bf = _mm(u, w_in, layer, COL_V, D_MODEL, TN_PROJ, tm, bf16_copy=True, slab=(v_dest, DEPTH, layer))
        y_sb = _sb_prompt(proj, k_bf, v_bf, lw["sb_bias"], u_ext, bsz, t)
    else:
        cache_k, cache_v, page_table = past
        k_new = _mm(u, w_in, layer, COL_K, D_MODEL, TN_PROJ, tm)
        v_new = _mm(u, w_in, layer, COL_V, D_MODEL, TN_PROJ, tm)
        y_sb = _sb_decode(proj, k_new, v_new, cache_k, cache_v, page_table, lw["sb_bias"], u_ext, layer, bsz, t)

    if state is None:
        ssd_buf = lru_buf = ssd_h0 = lru_h0 = None
    else:
        ssd_h0, ssd_buf, lru_h0, lru_buf = state
    y_ssd_raw, ssd_h = _ssd_scan(proj, dt_raw, bsz, t, rows, lw["ssd_conv_w"], lw["ssd_conv_b"], lw["ssd_dt_bias"],
                                 lw["ssd_a_log"], lw["ssd_d"], ssd_buf, ssd_h0)
    y_ssd = _ssd_gate(y_ssd_raw, proj, lw["ssd_norm_g"], tm)
    y_lru, lru_h = _lru(proj, bsz, t, rows, lw["lru_conv_w"], lw["lru_conv_b"], lw["lru_w_a"], lw["lru_w_x"], layer,
                        lw["lru_b_a"], lw["lru_b_x"], lw["lru_lambda"], lru_buf, lru_h0)
    merged = _merge(y_ssd, y_sb, y_lru, proj, lw["w_branch"], layer, tm, TN_MERGE)
    x = _mm_norm_res(merged, lw["w_out"], layer, x, lw["g_post_mix"], g1, tm, D_MODEL, bpb)
    u2 = _norm_mod(x, lw["g_pre_ffn"], sc2, sh2, tm, bpb)
    hidden = _mm_glu(u2, lw["w_up"], layer, tm, TN_GLU)
    x = _mm_norm_res(hidden, lw["w_down"], layer, x, lw["g_post_ffn"], g2, tm, TK_DOWN, bpb)

    proj3 = proj.reshape(bsz, t, N_MAIN)
    ssd_conv = proj3[:, t - (CONV_W - 1):, COL_XBC:COL_XBC + SSD_XBC]
    lru_conv = proj3[:, t - (CONV_W - 1):, COL_LX:COL_LX + D_MODEL]
    outs = (k_new, v_new, ssd_h.reshape(bsz, SSD_HEADS, SSD_HEAD_DIM, SSD_STATE), ssd_conv,
            lru_h.reshape(bsz, D_MODEL), lru_conv)
    return x, outs


def _row2(v):
    return v.reshape(1, -1)


def kernel(x_prompt, x_sample, c_prompt, c_sample, cache_k, cache_v, page_table, state_ssd, state_ssd_conv, state_lru, state_lru_conv, w_ada, b_ada, g_pre_mix, g_post_mix, g_pre_ffn, g_post_ffn, w_in, ssd_conv_w, ssd_conv_b, ssd_dt_bias, ssd_a_log, ssd_d, ssd_norm_g, sb_bias, lru_conv_w, lru_conv_b, lru_w_a, lru_b_a, lru_w_x, lru_b_x, lru_lambda, w_branch, w_out, w_up, w_down):
    bp, tp, d = x_prompt.shape
    bs, ts, _ = x_sample.shape
    depth = w_in.shape[0]
    n_pool, page = cache_k.shape[1], cache_k.shape[2]
    assert page == SB_BLOCK and tp % 512 == 0 and ts % 8 == 0 and ts < SB_BLOCK

    w_in_r = _repack_w_in(w_in)
    w_branch_b = w_branch.astype(BF16)
    w_out_b = w_out.astype(BF16)
    w_down_b = w_down.astype(BF16)
    lru_w_a_b = lru_w_a.astype(BF16)
    lru_w_x_b = lru_w_x.astype(BF16)
    u_ext = _suffix_matrix()
    cache_k2 = cache_k.reshape(depth * n_pool, page * SB_HEADS, SB_HEAD_DIM)
    cache_v2 = cache_v.reshape(depth * n_pool, page * SB_HEADS, SB_HEAD_DIM)

    xp = x_prompt.reshape(bp * tp, d)
    xs = x_sample.reshape(bs * ts, d)
    n_c = bp + bs
    n_c_pad = -(-n_c // 16) * 16
    c_all = jnp.concatenate([c_prompt, c_sample, jnp.zeros((n_c_pad - n_c, d), c_prompt.dtype)], axis=0)
    tm_p = 512
    cfg_p = (bp, tp, tm_p, tp // tm_p, 512)
    cfg_s = (bs, ts, bs * ts, 1, ts)

    per_layer = []
    kv_slabs = None
    for l in range(depth):
        lw = {
            "g_pre_mix": _row2(g_pre_mix[l]), "g_post_mix": _row2(g_post_mix[l]),
            "g_pre_ffn": _row2(g_pre_ffn[l]), "g_post_ffn": _row2(g_post_ffn[l]),
            "w_in": w_in_r, "ssd_conv_w": ssd_conv_w[l], "ssd_conv_b": _row2(ssd_conv_b[l]),
            "ssd_dt_bias": ssd_dt_bias[l], "ssd_a_log": ssd_a_log[l], "ssd_d": ssd_d[l],
            "ssd_norm_g": _row2(ssd_norm_g[l]), "sb_bias": sb_bias[l],
            "lru_conv_w": lru_conv_w[l], "lru_conv_b": _row2(lru_conv_b[l]),
            "lru_w_a": lru_w_a_b, "lru_w_x": lru_w_x_b, "lru_b_a": _row2(lru_b_a[l]), "lru_b_x": _row2(lru_b_x[l]),
            "lru_lambda": _row2(lru_lambda[l]), "w_branch": w_branch_b, "w_out": w_out_b,
            "w_up": w_up, "w_down": w_down_b,
        }
        mod = _mm(c_all, w_ada, l, 0, 6 * d, TN_PROJ, n_c_pad, bias=_row2(b_ada[l]), pre_silu=True)
        mods_p = tuple(mod[:bp, i * d:(i + 1) * d].reshape(bp, 1, d) for i in range(6))
        mods_s = tuple(jnp.repeat(mod[bp:n_c, i * d:(i + 1) * d], ts, axis=0).reshape(1, bs * ts, d) for i in range(6))
        xp, outs_p = _layer(xp, mods_p, cfg_p, lw, l, u_ext, kv_slabs=kv_slabs)
        kv_slabs = outs_p[:2]
        state = (state_ssd[l].reshape(bs, SSD_PAIRS, 2 * SSD_HEAD_DIM, SSD_STATE), state_ssd_conv[l],
                 state_lru[l].reshape(bs, 1, d), state_lru_conv[l])
        xs, outs_s = _layer(xs, mods_s, cfg_s, lw, l, u_ext, past=(cache_k2, cache_v2, page_table), state=state)
        per_layer.append((outs_p, outs_s))

    def stack(group, idx):
        return jnp.stack([per_layer[l][group][idx] for l in range(depth)])

    kv_p = (depth, bp, tp, SB_HEADS, SB_HEAD_DIM)
    kv_s = (depth, bs, ts, SB_HEADS, SB_HEAD_DIM)
    return (xp.reshape(bp, tp, d), xs.reshape(bs, ts, d),
            kv_slabs[0].reshape(kv_p), kv_slabs[1].reshape(kv_p), stack(1, 0).reshape(kv_s), stack(1, 1).reshape(kv_s),
            stack(0, 2), stack(1, 2), stack(0, 3), stack(1, 3),
            stack(0, 4), stack(1, 4), stack(0, 5), stack(1, 5))
```
